```python
import jax, jax.numpy as jnp
from jax import lax
import numpy as np

D_MODEL = 4096
BATCH = 2
SEQ = 4096
DEPTH = 1

N_META = 16
D_MIX = D_MODEL
HGRN_WIDTH = D_MIX // 2
LRU_WIDTH = D_MIX - HGRN_WIDTH
HGRN_HEAD_DIM = 128
HGRN_HEADS = HGRN_WIDTH // HGRN_HEAD_DIM
LRU_BLOCK = 256
LRU_HEADS = LRU_WIDTH // LRU_BLOCK
CONV_WIDTH = 4
RG_C = 8.0
CHUNK = 64
D_FF = ((8 * D_MODEL // 3 + 255) // 256) * 256
EPS = 1e-6
IN_COLS = 4 * HGRN_WIDTH + 2 * LRU_WIDTH

kernel_name = "hymba_hgrn2_rglru_macaron_block"


def rms_norm(x, gain, group=None):
    shp = x.shape
    xf = x.astype(jnp.float32)
    if group is not None:
        xf = xf.reshape(*shp[:-1], shp[-1] // group, group)
    y = xf * lax.rsqrt(jnp.mean(xf * xf, axis=-1, keepdims=True) + EPS)
    y = y.reshape(shp) * gain.astype(jnp.float32)
    return y.astype(x.dtype)


def swiglu(u, w_gate, w_up, w_down):
    return (jax.nn.silu(u @ w_gate) * (u @ w_up)) @ w_down


def _hgrn2_chunk(state, inp):
    q, k, v, log_f = inp
    c = q.shape[2]
    cum = jnp.cumsum(log_f, axis=2)
    o_inter = jnp.einsum('bhck,bhkv->bhcv', q * jnp.exp(cum), state)
    causal = jnp.tril(jnp.ones((c, c), dtype=bool))[None, None, :, :, None]
    diff = cum[:, :, :, None, :] - cum[:, :, None, :, :]
    decay = jnp.exp(jnp.where(causal, diff, -jnp.inf))
    scores = jnp.einsum('bhik,bhjk,bhijk->bhij', q, k, decay)
    o_intra = jnp.einsum('bhij,bhjv->bhiv', scores, v)
    last = cum[:, :, -1:, :]
    k_dec = k * jnp.exp(last - cum)
    new_state = jnp.exp(last[:, :, 0, :])[..., None] * state + jnp.einsum('bhck,bhcv->bhkv', k_dec, v)
    return new_state, o_inter + o_intra


def hgrn2_group(q, f_logit, i_in, g_out, lb, out_norm):
    b, t, _ = q.shape
    q = jax.nn.silu(q)
    log_f = jnp.log(lb + (1.0 - lb) * jax.nn.sigmoid(f_logit))
    k = (1.0 - lb) * jax.nn.sigmoid(-f_logit)

    def heads(a):
        return a.reshape(b, t, HGRN_HEADS, HGRN_HEAD_DIM).transpose(0, 2, 1, 3)

    q, k, v, log_f = heads(q), heads(k), heads(i_in), heads(log_f)
    state0 = jnp.zeros((b, HGRN_HEADS, HGRN_HEAD_DIM, HGRN_HEAD_DIM), dtype=v.dtype)
    state, o_meta = _hgrn2_chunk(state0, (q[:, :, :N_META], k[:, :, :N_META],
                                          v[:, :, :N_META], log_f[:, :, :N_META]))
    n_chunks = (t - N_META) // CHUNK

    def to_chunks(a):
        return a[:, :, N_META:].reshape(b, HGRN_HEADS, n_chunks, CHUNK, HGRN_HEAD_DIM).transpose(2, 0, 1, 3, 4)

    _, o_real = lax.scan(_hgrn2_chunk, state, (to_chunks(q), to_chunks(k), to_chunks(v), to_chunks(log_f)))
    o_real = o_real.transpose(1, 2, 0, 3, 4).reshape(b, HGRN_HEADS, n_chunks * CHUNK, HGRN_HEAD_DIM)
    o = jnp.concatenate([o_meta, o_real], axis=2).transpose(0, 2, 1, 3).reshape(b, t, HGRN_WIDTH)
    return rms_norm(o, out_norm, group=HGRN_HEAD_DIM) * jax.nn.silu(g_out)


def _linear_combine(left, right):
    a_l, b_l = left
    a_r, b_r = right
    return a_l * a_r, a_r * b_l + b_r


def rglru_group(x_br, g_br, conv_w, conv_b, w_a, b_a, w_x, b_x, lam, out_norm):
    b, t, _ = x_br.shape
    xc = lax.conv_general_dilated(
        x_br, conv_w[:, None, :].astype(x_br.dtype), window_strides=(1,),
        padding=[(CONV_WIDTH - 1, 0)], dimension_numbers=('NWC', 'WIO', 'NWC'),
        feature_group_count=LRU_WIDTH) + conv_b
    xh = xc.reshape(b, t, LRU_HEADS, LRU_BLOCK)
    r = jax.nn.sigmoid(jnp.einsum('bthi,hij->bthj', xh, w_a).reshape(b, t, LRU_WIDTH) + b_a)
    i = jax.nn.sigmoid(jnp.einsum('bthi,hij->bthj', xh, w_x).reshape(b, t, LRU_WIDTH) + b_x)
    log_a = -RG_C * r * jax.nn.softplus(-lam)
    a = jnp.exp(log_a)
    drive = jnp.sqrt(-jnp.expm1(2.0 * log_a)) * (i * xc)
    _, h = lax.associative_scan(_linear_combine, (a, drive), axis=1)
    return rms_norm(h, out_norm) * jax.nn.gelu(g_br)


def token_mixer(u, w_in, lb, hgrn_out_norm, conv_w, conv_b, w_a, b_a, w_x, b_x, lam, lru_out_norm, w_out):
    proj = u @ w_in
    splits = (HGRN_WIDTH, 2 * HGRN_WIDTH, 3 * HGRN_WIDTH, 4 * HGRN_WIDTH, 4 * HGRN_WIDTH + LRU_WIDTH)
    q, f_logit, i_in, g_out, x_br, g_br = jnp.split(proj, splits, axis=-1)
    y_hgrn = hgrn2_group(q, f_logit, i_in, g_out, lb.astype(u.dtype), hgrn_out_norm)
    y_lru = rglru_group(x_br, g_br, conv_w, conv_b, w_a, b_a, w_x, b_x, lam, lru_out_norm)
    return jnp.concatenate([y_hgrn, y_lru], axis=-1) @ w_out


def setup_inputs(seed: int = 0) -> dict:
    key = jax.random.key(seed)
    ks = jax.random.split(key, 26)
    f32 = jnp.float32

    def nrm(k, shape, scale):
        return jax.random.normal(k, shape, f32) * scale

    def gain(k, shape):
        return 1.0 + 0.02 * jax.random.normal(k, shape, f32)

    u = jax.random.uniform(ks[20], (DEPTH, LRU_WIDTH), f32, 0.9, 0.999)
    a_base = u ** (1.0 / RG_C)
    lam = jnp.log(a_base) - jnp.log1p(-a_base)
    return {
        "x": nrm(ks[0], (BATCH, SEQ, D_MODEL), 1.0),
        "meta_tokens": nrm(ks[1], (N_META, D_MODEL), 1.0),
        "ffn1_pre_norm": gain(ks[2], (DEPTH, D_MODEL)),
        "ffn1_w_gate": nrm(ks[3], (DEPTH, D_MODEL, D_FF), D_MODEL ** -0.5),
        "ffn1_w_up": nrm(ks[4], (DEPTH, D_MODEL, D_FF), D_MODEL ** -0.5),
        "ffn1_w_down": nrm(ks[5], (DEPTH, D_FF, D_MODEL), D_FF ** -0.5),
        "ffn1_post_norm": gain(ks[6], (DEPTH, D_MODEL)),
        "mix_pre_norm": gain(ks[7], (DEPTH, D_MODEL)),
        "w_in": nrm(ks[8], (DEPTH, D_MODEL, IN_COLS), D_MODEL ** -0.5),
        "hgrn_lb_logits": nrm(ks[9], (DEPTH + 1, HGRN_WIDTH), 0.5),
        "hgrn_out_norm": gain(ks[10], (DEPTH, HGRN_WIDTH)),
        "lru_conv_w": nrm(ks[11], (DEPTH, CONV_WIDTH, LRU_WIDTH), CONV_WIDTH ** -0.5),
        "lru_conv_b": nrm(ks[12], (DEPTH, LRU_WIDTH), 0.01),
        "lru_w_a": nrm(ks[13], (DEPTH, LRU_HEADS, LRU_BLOCK, LRU_BLOCK), LRU_BLOCK ** -0.5),
        "lru_b_a": nrm(ks[14], (DEPTH, LRU_WIDTH), 0.01),
        "lru_w_x": nrm(ks[15], (DEPTH, LRU_HEADS, LRU_BLOCK, LRU_BLOCK), LRU_BLOCK ** -0.5),
        "lru_b_x": nrm(ks[16], (DEPTH, LRU_WIDTH), 0.01),
        "lru_lambda": lam,
        "lru_out_norm": gain(ks[17], (DEPTH, LRU_WIDTH)),
        "w_out": nrm(ks[18], (DEPTH, D_MIX, D_MODEL), D_MIX ** -0.5),
        "mix_post_norm": gain(ks[19], (DEPTH, D_MODEL)),
        "ffn2_pre_norm": gain(ks[21], (DEPTH, D_MODEL)),
        "ffn2_w_gate": nrm(ks[22], (DEPTH, D_MODEL, D_FF), D_MODEL ** -0.5),
        "ffn2_w_up": nrm(ks[23], (DEPTH, D_MODEL, D_FF), D_MODEL ** -0.5),
        "ffn2_w_down": nrm(ks[24], (DEPTH, D_FF, D_MODEL), D_FF ** -0.5),
        "ffn2_post_norm": gain(ks[25], (DEPTH, D_MODEL)),
    }


def reference(x, meta_tokens, ffn1_pre_norm, ffn1_w_gate, ffn1_w_up, ffn1_w_down, ffn1_post_norm,
              mix_pre_norm, w_in, hgrn_lb_logits, hgrn_out_norm, lru_conv_w, lru_conv_b,
              lru_w_a, lru_b_a, lru_w_x, lru_b_x, lru_lambda, lru_out_norm, w_out, mix_post_norm,
              ffn2_pre_norm, ffn2_w_gate, ffn2_w_up, ffn2_w_down, ffn2_post_norm):
    b = x.shape[0]
    meta = jnp.broadcast_to(meta_tokens[None].astype(x.dtype), (b, N_META, D_MODEL))
    h = jnp.concatenate([meta, x], axis=1)
    lb_all = jnp.cumsum(jax.nn.softmax(hgrn_lb_logits.astype(jnp.float32), axis=0), axis=0)
    for l in range(DEPTH):
        f1 = swiglu(rms_norm(h, ffn1_pre_norm[l]), ffn1_w_gate[l], ffn1_w_up[l], ffn1_w_down[l])
        h = h + 0.5 * rms_norm(f1, ffn1_post_norm[l])
        m = token_mixer(rms_norm(h, mix_pre_norm[l]), w_in[l], lb_all[l], hgrn_out_norm[l],
                        lru_conv_w[l], lru_conv_b[l], lru_w_a[l], lru_b_a[l], lru_w_x[l], lru_b_x[l],
                        lru_lambda[l], lru_out_norm[l], w_out[l])
        h = h + rms_norm(m, mix_post_norm[l])
        f2 = swiglu(rms_norm(h, ffn2_pre_norm[l]), ffn2_w_gate[l], ffn2_w_up[l], ffn2_w_down[l])
        h = h + 0.5 * rms_norm(f2, ffn2_post_norm[l])
    return h[:, N_META:]
```

```python
import functools

import jax
import jax.numpy as jnp
from jax import lax
from jax.experimental import pallas as pl
from jax.experimental.pallas import tpu as pltpu

F32 = jnp.float32
BF16 = jnp.bfloat16

EPS = 1e-6
RG_C = 8.0
HEAD = 128
CHUNK = 64
SUBLANES = 8
V7X_VMEM_LIMIT_BYTES = 56 * 1024 * 1024


def _tile(n, want):
    t = min(n, want)
    while n % t:
        t -= 1
    return t


def _rms(x, gain):
    return x * lax.rsqrt(jnp.mean(x * x, axis=-1, keepdims=True) + EPS) * gain


def _params(*sem):
    return pltpu.CompilerParams(dimension_semantics=sem, vmem_limit_bytes=V7X_VMEM_LIMIT_BYTES)


def _ffn_kernel(h_ref, gpre_ref, wg_ref, wu_ref, wd_ref, gpost_ref, o_ref, u_scr, *, col_chunk):
    j = pl.program_id(1)

    @pl.when(j == 0)
    def _():
        u_scr[...] = _rms(h_ref[...], gpre_ref[...]).astype(u_scr.dtype)
        o_ref[...] = jnp.zeros_like(o_ref)

    u = u_scr[...]
    g = jnp.dot(u, wg_ref[...], preferred_element_type=F32)
    v = jnp.dot(u, wu_ref[...], preferred_element_type=F32)
    a = (g * jax.nn.sigmoid(g) * v).astype(BF16)
    d = o_ref.shape[1]
    for c0 in range(0, d, col_chunk):
        o_ref[:, c0:c0 + col_chunk] += jnp.dot(a, wd_ref[:, c0:c0 + col_chunk], preferred_element_type=F32)

    @pl.when(j == pl.num_programs(1) - 1)
    def _():
        o_ref[...] = h_ref[...] + 0.5 * _rms(o_ref[...], gpost_ref[...])


def _ffn(h, gpre, wg, wu, wd, gpost, *, bm_want=512, bf_want=256):
    m, d = h.shape
    f = wg.shape[1]
    bm = _tile(m, bm_want)
    bf = _tile(f, bf_want)
    kern = functools.partial(_ffn_kernel, col_chunk=_tile(d, 512))
    return pl.pallas_call(
        kern,
        grid=(m // bm, f // bf),
        in_specs=[
            pl.BlockSpec((bm, d), lambda i, j: (i, 0), pipeline_mode=pl.Buffered(1)),
            pl.BlockSpec((1, d), lambda i, j: (0, 0)),
            pl.BlockSpec((d, bf), lambda i, j: (0, j)),
            pl.BlockSpec((d, bf), lambda i, j: (0, j)),
            pl.BlockSpec((bf, d), lambda i, j: (j, 0)),
            pl.BlockSpec((1, d), lambda i, j: (0, 0)),
        ],
        out_specs=pl.BlockSpec((bm, d), lambda i, j: (i, 0)),
        out_shape=jax.ShapeDtypeStruct((m, d), F32),
        scratch_shapes=[pltpu.VMEM((bm, d), BF16)],
        compiler_params=_params("parallel", "arbitrary"),
        name="ffn",
    )(h, gpre, wg, wu, wd, gpost)


def _inproj_kernel(h_ref, g_ref, w_ref, o_ref, u_scr):
    @pl.when(pl.program_id(1) == 0)
    def _():
        u_scr[...] = _rms(h_ref[...], g_ref[...]).astype(u_scr.dtype)

    o_ref[...] = jnp.dot(u_scr[...], w_ref[...], preferred_element_type=F32)


def _inproj(h, gain, w, *, bm_want=512, bn_want=512):
    m, d = h.shape
    n = w.shape[1]
    bm = _tile(m, bm_want)
    bn = _tile(n, bn_want)
    return pl.pallas_call(
        _inproj_kernel,
        grid=(m // bm, n // bn),
        in_specs=[
            pl.BlockSpec((bm, d), lambda i, j: (i, 0)),
            pl.BlockSpec((1, d), lambda i, j: (0, 0)),
            pl.BlockSpec((d, bn), lambda i, j: (0, j)),
        ],
        out_specs=pl.BlockSpec((bm, bn), lambda i, j: (i, j)),
        out_shape=jax.ShapeDtypeStruct((m, n), F32),
        scratch_shapes=[pltpu.VMEM((bm, d), BF16)],
        compiler_params=_params("parallel", "arbitrary"),
        name="inproj",
    )(h, gain, w)


def _outproj_kernel(yh_ref, yl_ref, w_ref, h_ref, g_ref, o_ref, *, nkh):
    k = pl.program_id(1)

    @pl.when(k == 0)
    def _():
        o_ref[...] = jnp.zeros_like(o_ref)

    @pl.when(k < nkh)
    def _():
        o_ref[...] += jnp.dot(yh_ref[...], w_ref[...], preferred_element_type=F32)

    @pl.when(k >= nkh)
    def _():
        o_ref[...] += jnp.dot(yl_ref[...], w_ref[...], preferred_element_type=F32)

    @pl.when(k == pl.num_programs(1) - 1)
    def _():
        o_ref[...] = h_ref[...] + _rms(o_ref[...], g_ref[...])


def _outproj(yh, yl, w, h, gain, *, bm_want=512, bk_want=512):
    m, kh = yh.shape
    d = w.shape[1]
    bm = _tile(m, bm_want)
    bk = _tile(kh, bk_want)
    nkh = kh // bk
    return pl.pallas_call(
        functools.partial(_outproj_kernel, nkh=nkh),
        grid=(m // bm, 2 * nkh),
        in_specs=[
            pl.BlockSpec((bm, bk), lambda i, k: (i, jnp.minimum(k, nkh - 1))),
            pl.BlockSpec((bm, bk), lambda i, k: (i, jnp.maximum(k - nkh, 0))),
            pl.BlockSpec((bk, d), lambda i, k: (k, 0)),
            pl.BlockSpec((bm, d), lambda i, k: (i, 0), pipeline_mode=pl.Buffered(1)),
            pl.BlockSpec((1, d), lambda i, k: (0, 0)),
        ],
        out_specs=pl.BlockSpec((bm, d), lambda i, k: (i, 0)),
        out_shape=jax.ShapeDtypeStruct((m, d), F32),
        compiler_params=_params("parallel", "arbitrary"),
        name="outproj",
    )(yh, yl, w, h, gain)


def _cumsum_rows(x):
    r, w = x.shape
    row = lax.broadcasted_iota(jnp.int32, (SUBLANES, w), 0)
    out = []
    carry = None
    for g in range(r // SUBLANES):
        v = x[g * SUBLANES:(g + 1) * SUBLANES]
        for dd in (1, 2, 4):
            v = v + jnp.where(row >= dd, pltpu.roll(v, dd, 0), 0.0)
        if carry is not None:
            v = v + carry
        carry = jnp.broadcast_to(v[SUBLANES - 1:SUBLANES], (SUBLANES, w))
        out.append(v)
    return jnp.concatenate(out, axis=0)


def _hgrn_gates(q_raw, f_raw, lb):
    q = q_raw * jax.nn.sigmoid(q_raw)
    sf = jax.nn.sigmoid(f_raw)
    fg = lb + (1.0 - lb) * sf
    k = (1.0 - lb) * (1.0 - sf)
    return q, k, jnp.log(fg)


def _level_reference(cum, s):
    r, w = cum.shape
    if s >= SUBLANES:
        parts = [jnp.broadcast_to(cum[p + s - 1:p + s], (2 * s, w)) for p in range(0, r, 2 * s)]
        return jnp.concatenate(parts, axis=0)
    row = lax.broadcasted_iota(jnp.int32, (SUBLANES, w), 0)
    parts = []
    for g in range(r // SUBLANES):
        v = cum[g * SUBLANES:(g + 1) * SUBLANES]
        if s == 4:
            ref = jnp.broadcast_to(v[3:4], (SUBLANES, w))
        elif s == 2:
            ref = jnp.where(row < 4, jnp.broadcast_to(v[1:2], (SUBLANES, w)),
                            jnp.broadcast_to(v[5:6], (SUBLANES, w)))
        else:
            ref = jnp.where((row & 1) == 1, pltpu.roll(v, 1, 0), v)
        parts.append(ref)
    return jnp.concatenate(parts, axis=0)


_NT = (((1,), (1,)), ((), ()))
_TN = (((0,), (0,)), ((), ()))


def _hgrn_kernel(q_ref, f_ref, i_ref, g_ref, lb_ref, gain_ref, s0_ref, o_ref, st_scr, *, hb, tb):
    @pl.when(pl.program_id(2) == 0)
    def _():
        st_scr[...] = s0_ref[...]

    w = hb * HEAD
    lb = lb_ref[...]
    gain = gain_ref[...]
    ri = lax.broadcasted_iota(jnp.int32, (CHUNK, CHUNK), 0)
    ci = lax.broadcasted_iota(jnp.int32, (CHUNK, CHUNK), 1)
    roww = lax.broadcasted_iota(jnp.int32, (CHUNK, w), 0)
    levels = (32, 16, 8, 4, 2, 1)
    diag = ri == ci
    same_block = [(ri & -(2 * s)) == (ci & -(2 * s)) for s in levels]

    def chunk(c, carry):
        r0 = pl.multiple_of(c * CHUNK, CHUNK)
        q, k, lf = _hgrn_gates(q_ref[pl.ds(r0, CHUNK), :], f_ref[pl.ds(r0, CHUNK), :], lb)
        v = i_ref[pl.ds(r0, CHUNK), :]
        graw = g_ref[pl.ds(r0, CHUNK), :]
        cum = _cumsum_rows(lf)
        last = jnp.broadcast_to(cum[CHUNK - 1:CHUNK], (CHUNK, w))
        q_in = (q * jnp.exp(cum)).astype(BF16)
        k_out = (k * jnp.exp(last - cum)).astype(BF16)
        vb = v.astype(BF16)
        qb = q.astype(BF16)
        kb = k.astype(BF16)
        qs, ks = [], []
        for s in levels:
            second = (roww & s) != 0
            ref = _level_reference(cum, s)
            e = jnp.exp(jnp.where(second, cum - ref, ref - cum))
            qk = jnp.where(second, q, k) * e
            qs.append(jnp.where(second, qk, 0.0).astype(BF16))
            ks.append(jnp.where(second, 0.0, qk).astype(BF16))
        outs = []
        for h in range(hb):
            sl = slice(h * HEAD, (h + 1) * HEAD)
            a = jnp.where(diag, lax.dot_general(qb[:, sl], kb[:, sl], _NT, preferred_element_type=F32), 0.0)
            for same, qt, kt in zip(same_block, qs, ks):
                p = lax.dot_general(qt[:, sl], kt[:, sl], _NT, preferred_element_type=F32)
                a = a + jnp.where(same, p, 0.0)
            st = st_scr[h]
            o = lax.dot_general(q_in[:, sl], st.astype(BF16), _NT, preferred_element_type=F32)
            o = o + jnp.dot(a.astype(BF16), vb[:, sl], preferred_element_type=F32)
            st_scr[h] = st * jnp.exp(last[0:1, sl]) + lax.dot_general(
                vb[:, sl], k_out[:, sl], _TN, preferred_element_type=F32)
            outs.append(o * lax.rsqrt(jnp.mean(o * o, axis=-1, keepdims=True) + EPS))
        y = jnp.concatenate(outs, axis=1) * gain * (graw * jax.nn.sigmoid(graw))
        o_ref[pl.ds(r0, CHUNK), :] = y.astype(o_ref.dtype)
        return carry

    lax.fori_loop(0, tb // CHUNK, chunk, 0)


def _hgrn(proj, lb, gain, s0, *, nb, hw, hb_want=4, tb_want=256):
    m = proj.shape[0]
    t = m // nb
    nh = hw // HEAD
    hb = _tile(nh, hb_want)
    w = hb * HEAD
    tb = _tile(t, tb_want)
    nt = t // tb
    ng = nh // hb

    def col(group):
        return lambda b, hg, tt: (b * nt + tt, group * ng + hg)

    return pl.pallas_call(
        functools.partial(_hgrn_kernel, hb=hb, tb=tb),
        grid=(nb, ng, nt),
        in_specs=[
            pl.BlockSpec((tb, w), col(0)),
            pl.BlockSpec((tb, w), col(1)),
            pl.BlockSpec((tb, w), col(2)),
            pl.BlockSpec((tb, w), col(3)),
            pl.BlockSpec((1, w), lambda b, hg, tt: (0, hg)),
            pl.BlockSpec((1, w), lambda b, hg, tt: (0, hg)),
            pl.BlockSpec((hb, HEAD, HEAD), lambda b, hg, tt: (hg, 0, 0)),
        ],
        out_specs=pl.BlockSpec((tb, w), lambda b, hg, tt: (b * nt + tt, hg)),
        out_shape=jax.ShapeDtypeStruct((m, hw), BF16),
        scratch_shapes=[pltpu.VMEM((hb, HEAD, HEAD), F32)],
        compiler_params=_params("parallel", "parallel", "arbitrary"),
        name="hgrn",
    )(proj, proj, proj, proj, lb, gain, s0)


def _hgrn_meta_kernel(f_ref, i_ref, lb_ref, s_ref, *, nh):
    f_raw = f_ref[...]
    _, k, lf = _hgrn_gates(f_raw, f_raw, lb_ref[...])
    cum = _cumsum_rows(lf)
    r, w = cum.shape
    last = jnp.broadcast_to(cum[r - 1:r], (r, w))
    k_out = (k * jnp.exp(last - cum)).astype(BF16)
    vb = i_ref[...].astype(BF16)
    for h in range(nh):
        sl = slice(h * HEAD, (h + 1) * HEAD)
        s_ref[h] = lax.dot_general(vb[:, sl], k_out[:, sl], _TN, preferred_element_type=F32)


def _hgrn_meta(proj_meta, lb, *, hw):
    nmeta = proj_meta.shape[0]
    nh = hw // HEAD
    return pl.pallas_call(
        functools.partial(_hgrn_meta_kernel, nh=nh),
        grid=(1,),
        in_specs=[
            pl.BlockSpec((nmeta, hw), lambda i: (0, 1)),
            pl.BlockSpec((nmeta, hw), lambda i: (0, 2)),
            pl.BlockSpec((1, hw), lambda i: (0, 0)),
        ],
        out_specs=pl.BlockSpec((nh, HEAD, HEAD), lambda i: (0, 0, 0)),
        out_shape=jax.ShapeDtypeStruct((nh, HEAD, HEAD), F32),
        compiler_params=_params("arbitrary"),
        name="hgrn_meta",
    )(proj_meta, proj_meta, lb)


def _lru_kernel(x_ref, g_ref, cw_ref, cb_ref, wa_ref, ba_ref, wx_ref, bx_ref, lam_ref, gain_ref, h0_ref, xt0_ref,
                y_ref, hl_ref, xbuf, a_scr, b_scr, hcar, *, tb, nblk, blk):
    t = pl.program_id(1)

    @pl.when(t == 0)
    def _():
        hcar[...] = h0_ref[...]
        xbuf[0:SUBLANES, :] = xt0_ref[...]

    x = x_ref[...]
    c = x.shape[1]
    xbuf[SUBLANES:SUBLANES + tb, :] = x
    cw = cw_ref[...]
    xc = (cb_ref[...] + cw[3:4] * x
          + cw[2:3] * xbuf[SUBLANES - 1:SUBLANES - 1 + tb, :]
          + cw[1:2] * xbuf[SUBLANES - 2:SUBLANES - 2 + tb, :]
          + cw[0:1] * xbuf[SUBLANES - 3:SUBLANES - 3 + tb, :])
    xbuf[0:SUBLANES, :] = x[tb - SUBLANES:tb, :]

    xcb = xc.astype(BF16)
    nlam = -lam_ref[...]
    sp = jnp.maximum(nlam, 0.0) + jnp.log(1.0 + jnp.exp(-jnp.abs(nlam)))
    for hh in range(nblk):
        sl = slice(hh * blk, (hh + 1) * blk)
        r = jax.nn.sigmoid(jnp.dot(xcb[:, sl], wa_ref[hh], preferred_element_type=F32) + ba_ref[:, sl])
        ig = jax.nn.sigmoid(jnp.dot(xcb[:, sl], wx_ref[hh], preferred_element_type=F32) + bx_ref[:, sl])
        a = jnp.exp((-RG_C) * r * sp[:, sl])
        a_scr[:, sl] = a
        b_scr[:, sl] = jnp.sqrt(1.0 - a * a) * (ig * xc[:, sl])

    row = lax.broadcasted_iota(jnp.int32, (SUBLANES, c), 0)

    def group(gi, hprev):
        r0 = pl.multiple_of(gi * SUBLANES, SUBLANES)
        av = a_scr[pl.ds(r0, SUBLANES), :]
        bv = b_scr[pl.ds(r0, SUBLANES), :]
        for dd in (1, 2, 4):
            keep = row >= dd
            bv = jnp.where(keep, av * pltpu.roll(bv, dd, 0) + bv, bv)
            av = jnp.where(keep, av * pltpu.roll(av, dd, 0), av)
        hv = av * hprev + bv
        b_scr[pl.ds(r0, SUBLANES), :] = hv
        return jnp.broadcast_to(hv[SUBLANES - 1:SUBLANES], (SUBLANES, c))

    hlast = lax.fori_loop(0, tb // SUBLANES, group, hcar[...])
    hcar[...] = hlast

    h = b_scr[...]
    gr = g_ref[...]
    gelu = 0.5 * gr * (1.0 + jnp.tanh(0.7978845608028654 * (gr + 0.044715 * (gr * gr * gr))))
    y_ref[...] = (_rms(h, gain_ref[...]) * gelu).astype(y_ref.dtype)

    @pl.when(t == pl.num_programs(1) - 1)
    def _():
        hl_ref[0] = hlast


def _lru(proj, cw, cb, wa, ba, wx, bx, lam, gain, h0, xt0, *, nb, lw, xcol, tb_want=256):
    m = proj.shape[0]
    t = m // nb
    tb = _tile(t, tb_want)
    nt = t // tb
    nblk, blk, _ = wa.shape
    vec = pl.BlockSpec((1, lw), lambda b, tt: (0, 0))
    return pl.pallas_call(
        functools.partial(_lru_kernel, tb=tb, nblk=nblk, blk=blk),
        grid=(nb, nt),
        in_specs=[
            pl.BlockSpec((tb, lw), lambda b, tt: (b * nt + tt, xcol)),
            pl.BlockSpec((tb, lw), lambda b, tt: (b * nt + tt, xcol + 1)),
            pl.BlockSpec((cw.shape[0], lw), lambda b, tt: (0, 0)),
            vec,
            pl.BlockSpec((nblk, blk, blk), lambda b, tt: (0, 0, 0)),
            vec,
            pl.BlockSpec((nblk, blk, blk), lambda b, tt: (0, 0, 0)),
            vec, vec, vec,
            pl.BlockSpec((SUBLANES, lw), lambda b, tt: (0, 0)),
            pl.BlockSpec((SUBLANES, lw), lambda b, tt: (0, 0)),
        ],
        out_specs=[
            pl.BlockSpec((tb, lw), lambda b, tt: (b * nt + tt, 0)),
            pl.BlockSpec((1, SUBLANES, lw), lambda b, tt: (b, 0, 0)),
        ],
        out_shape=[
            jax.ShapeDtypeStruct((m, lw), BF16),
            jax.ShapeDtypeStruct((nb, SUBLANES, lw), F32),
        ],
        scratch_shapes=[
            pltpu.VMEM((SUBLANES + tb, lw), F32),
            pltpu.VMEM((tb, lw), F32),
            pltpu.VMEM((tb, lw), F32),
            pltpu.VMEM((SUBLANES, lw), F32),
        ],
        compiler_params=_params("parallel", "arbitrary"),
        name="lru",
    )(proj, proj, cw, cb, wa, ba, wx, bx, lam, gain, h0, xt0)


def kernel(x, meta_tokens, ffn1_pre_norm, ffn1_w_gate, ffn1_w_up, ffn1_w_down, ffn1_post_norm, mix_pre_norm, w_in, hgrn_lb_logits, hgrn_out_norm, lru_conv_w, lru_conv_b, lru_w_a, lru_b_a, lru_w_x, lru_b_x, lru_lambda, lru_out_norm, w_out, mix_post_norm, ffn2_pre_norm, ffn2_w_gate, ffn2_w_up, ffn2_w_down, ffn2_post_norm):
    nb, seq, d = x.shape
    hw = hgrn_out_norm.shape[1]
    lw = lru_out_norm.shape[1]
    assert ffn1_pre_norm.shape[0] == 1, "one layer"
    assert hw == lw and w_in.shape[2] == 4 * hw + 2 * lw and hw % HEAD == 0
    assert seq % CHUNK == 0 and meta_tokens.shape[0] % SUBLANES == 0

    def vec(p):
        return p[0].reshape(1, -1).astype(F32)

    lb = jnp.cumsum(jax.nn.softmax(hgrn_lb_logits.astype(F32), axis=0), axis=0)[0].reshape(1, hw)
    w_in_b = w_in[0].astype(BF16)
    w_out_b = w_out[0].astype(BF16)
    wa_b = lru_w_a[0].astype(BF16)
    wx_b = lru_w_x[0].astype(BF16)
    ffn1 = (vec(ffn1_pre_norm), ffn1_w_gate[0].astype(BF16), ffn1_w_up[0].astype(BF16),
            ffn1_w_down[0].astype(BF16), vec(ffn1_post_norm))
    ffn2 = (vec(ffn2_pre_norm), ffn2_w_gate[0].astype(BF16), ffn2_w_up[0].astype(BF16),
            ffn2_w_down[0].astype(BF16), vec(ffn2_post_norm))
    lru_p = (lru_conv_w[0].astype(F32), vec(lru_conv_b), wa_b, vec(lru_b_a), wx_b, vec(lru_b_x),
             vec(lru_lambda), vec(lru_out_norm))
    xcol = (4 * hw) // lw

    hm = _ffn(meta_tokens.astype(F32), *ffn1)
    pm = _inproj(hm, vec(mix_pre_norm), w_in_b)
    s_meta = _hgrn_meta(pm, lb, hw=hw)
    zeros8 = jnp.zeros((SUBLANES, lw), F32)
    _, h_meta = _lru(pm, *lru_p, zeros8, zeros8, nb=1, lw=lw, xcol=xcol)
    xt_meta = pm[pm.shape[0] - SUBLANES:, 4 * hw:4 * hw + lw]

    h = x.reshape(nb * seq, d).astype(F32)
    h = _ffn(h, *ffn1)
    proj = _inproj(h, vec(mix_pre_norm), w_in_b)
    y_h = _hgrn(proj, lb, vec(hgrn_out_norm), s_meta, nb=nb, hw=hw)
    y_l, _ = _lru(proj, *lru_p, h_meta[0], xt_meta, nb=nb, lw=lw, xcol=xcol)
    h = _outproj(y_h, y_l, w_out_b, h, vec(mix_post_norm))
    h = _ffn(h, *ffn2)
    return h.reshape(nb, seq, d).astype(x.dtype)
```

```python
import functools

import jax
import jax.numpy as jnp
from jax import lax
from jax.experimental import pallas as pl
from jax.experimental.pallas import tpu as pltpu

F32 = jnp.float32
BF16 = jnp.bfloat16

EPS = 1e-6
RG_C = 8.0
HEAD = 128
CHUNK = 64
SUBLANES = 8
V7X_VMEM_LIMIT_BYTES = 56 * 1024 * 1024


def _tile(n, want):
    t = min(n, want)
    while n % t:
        t -= 1
    return t


def _rms(x, gain):
    return x * lax.rsqrt(jnp.mean(x * x, axis=-1, keepdims=True) + EPS) * gain


def _params(*sem):
    return pltpu.CompilerParams(dimension_semantics=sem, vmem_limit_bytes=V7X_VMEM_LIMIT_BYTES)


def _ffn_kernel(*refs, col_chunk, emit, n_side):
    h_ref, gpre_ref, wg_ref, wu_ref, wd_ref, gpost_ref = refs[:6]
    side_in = refs[6:6 + n_side]
    o_ref = refs[6 + n_side]
    emit_out = refs[7 + n_side:7 + n_side + (3 if emit else 0)]
    side_out = refs[7 + n_side + (3 if emit else 0):7 + 2 * n_side + (3 if emit else 0)]
    u_scr = refs[-1]
    j = pl.program_id(1)

    @pl.when(j == 0)
    def _():
        u_scr[...] = _rms(h_ref[...], gpre_ref[...]).astype(u_scr.dtype)
        o_ref[...] = jnp.zeros_like(o_ref)

    for src, dst in zip(side_in, side_out):
        dst[...] = src[...].astype(dst.dtype)

    if emit:
        wg = wg_ref[...].astype(BF16)
        wu = wu_ref[...].astype(BF16)
        emit_out[0][...] = wg
        emit_out[1][...] = wu
        emit_out[2][...] = wd_ref[...].astype(BF16)
        wd_src = emit_out[2]
    else:
        wg = wg_ref[...]
        wu = wu_ref[...]
        wd_src = wd_ref
    u = u_scr[...]
    g = jnp.dot(u, wg, preferred_element_type=F32)
    v = jnp.dot(u, wu, preferred_element_type=F32)
    a = (g * jax.nn.sigmoid(g) * v).astype(BF16)
    d = o_ref.shape[1]
    for c0 in range(0, d, col_chunk):
        o_ref[:, c0:c0 + col_chunk] += jnp.dot(a, wd_src[:, c0:c0 + col_chunk], preferred_element_type=F32)

    @pl.when(j == pl.num_programs(1) - 1)
    def _():
        o_ref[...] = h_ref[...] + 0.5 * _rms(o_ref[...], gpost_ref[...])


def _ffn(h, gpre, wg, wu, wd, gpost, side=(), *, bm_want=512, bf_want=256):
    m, d = h.shape
    f = wg.shape[1]
    bm = _tile(m, bm_want)
    bf = _tile(f, bf_want)
    ni, nj = m // bm, f // bf
    emit = wg.dtype != BF16
    kern = functools.partial(_ffn_kernel, col_chunk=_tile(d, 512), emit=emit, n_side=len(side))
    in_specs = [
        pl.BlockSpec((bm, d), lambda i, j: (i, 0), pipeline_mode=pl.Buffered(1)),
        pl.BlockSpec((1, d), lambda i, j: (0, 0)),
        pl.BlockSpec((d, bf), lambda i, j: (0, j)),
        pl.BlockSpec((d, bf), lambda i, j: (0, j)),
        pl.BlockSpec((bf, d), lambda i, j: (j, 0)),
        pl.BlockSpec((1, d), lambda i, j: (0, 0)),
    ]
    out_specs = [pl.BlockSpec((bm, d), lambda i, j: (i, 0))]
    out_shape = [jax.ShapeDtypeStruct((m, d), F32)]
    if emit:
        assert ni == 1, "weights are converted on a single pass over the hidden tiles"
        out_specs += [pl.BlockSpec((d, bf), lambda i, j: (0, j)), pl.BlockSpec((d, bf), lambda i, j: (0, j)),
                      pl.BlockSpec((bf, d), lambda i, j: (j, 0))]
        out_shape += [jax.ShapeDtypeStruct(w.shape, BF16) for w in (wg, wu, wd)]
    side_specs = []
    for arr, transposed in side:
        r, c = arr.shape
        if transposed:
            assert r % nj == 0 and c % ni == 0
            side_specs.append(pl.BlockSpec((r // nj, c // ni), lambda i, j: (j, i)))
        else:
            assert r % ni == 0 and c % nj == 0
            side_specs.append(pl.BlockSpec((r // ni, c // nj), lambda i, j: (i, j)))
        out_shape.append(jax.ShapeDtypeStruct(arr.shape, BF16))
    res = pl.pallas_call(
        kern,
        grid=(ni, nj),
        in_specs=in_specs + side_specs,
        out_specs=out_specs + side_specs,
        out_shape=out_shape,
        scratch_shapes=[pltpu.VMEM((bm, d), BF16)],
        compiler_params=_params("parallel", "arbitrary"),
        name="ffn_convert" if emit else "ffn",
    )(h, gpre, wg, wu, wd, gpost, *[arr for arr, _ in side])
    out = res[0]
    weights = tuple(res[1:4]) if emit else (wg, wu, wd)
    return out, weights, tuple(res[len(res) - len(side):]) if side else ()


def _inproj_kernel(*refs, emit, has_side):
    h_ref, g_ref, w_ref = refs[:3]
    k = 3
    side_in = refs[k] if has_side else None
    k += has_side
    o_ref = refs[k]
    k += 1
    wb_ref = refs[k] if emit else None
    k += emit
    side_out = refs[k] if has_side else None
    u_scr = refs[-1]

    @pl.when(pl.program_id(1) == 0)
    def _():
        u_scr[...] = _rms(h_ref[...], g_ref[...]).astype(u_scr.dtype)
        if has_side:
            side_out[...] = side_in[...].astype(side_out.dtype)

    if emit:
        w = w_ref[...].astype(BF16)
        wb_ref[...] = w
    else:
        w = w_ref[...]
    o_ref[...] = jnp.dot(u_scr[...], w, preferred_element_type=F32)


def _inproj(h, gain, w, side=None, *, bm_want=512, bn_want=1024):
    m, d = h.shape
    n = w.shape[1]
    bm = _tile(m, bm_want)
    bn = _tile(n, bn_want)
    ni = m // bm
    emit = w.dtype != BF16
    has_side = side is not None
    in_specs = [
        pl.BlockSpec((bm, d), lambda i, j: (i, 0), pipeline_mode=pl.Buffered(1)),
        pl.BlockSpec((1, d), lambda i, j: (0, 0)),
        pl.BlockSpec((d, bn), lambda i, j: (0, j)),
    ]
    out_specs = [pl.BlockSpec((bm, bn), lambda i, j: (i, j))]
    out_shape = [jax.ShapeDtypeStruct((m, n), F32)]
    if emit:
        assert ni == 1, "weights are converted on a single pass over the column tiles"
        out_specs.append(pl.BlockSpec((d, bn), lambda i, j: (0, j)))
        out_shape.append(jax.ShapeDtypeStruct(w.shape, BF16))
    args = [h, gain, w]
    if has_side:
        r, c = side.shape
        assert r % ni == 0
        spec = pl.BlockSpec((r // ni, c), lambda i, j: (i, 0))
        in_specs.append(spec)
        out_specs.append(spec)
        out_shape.append(jax.ShapeDtypeStruct(side.shape, BF16))
        args.append(side)
    res = pl.pallas_call(
        functools.partial(_inproj_kernel, emit=emit, has_side=has_side),
        grid=(ni, n // bn),
        in_specs=in_specs,
        out_specs=out_specs,
        out_shape=out_shape,
        scratch_shapes=[pltpu.VMEM((bm, d), BF16)],
        compiler_params=_params("parallel", "arbitrary"),
        name="inproj_convert" if emit else "inproj",
    )(*args)
    return res[0], (res[1] if emit else w), (res[-1] if has_side else None)


def _outproj_kernel(yh_ref, yl_ref, w_ref, h_ref, g_ref, o_ref, *, nkh):
    k = pl.program_id(1)

    @pl.when(k == 0)
    def _():
        o_ref[...] = jnp.zeros_like(o_ref)

    @pl.when(k < nkh)
    def _():
        o_ref[...] += jnp.dot(yh_ref[...], w_ref[...], preferred_element_type=F32)

    @pl.when(k >= nkh)
    def _():
        o_ref[...] += jnp.dot(yl_ref[...], w_ref[...], preferred_element_type=F32)

    @pl.when(k == pl.num_programs(1) - 1)
    def _():
        o_ref[...] = h_ref[...] + _rms(o_ref[...], g_ref[...])


def _outproj(yh, yl, w, h, gain, *, bm_want=512, bk_want=1024):
    m, kh = yh.shape
    d = w.shape[1]
    bm = _tile(m, bm_want)
    bk = _tile(kh, bk_want)
    nkh = kh // bk
    return pl.pallas_call(
        functools.partial(_outproj_kernel, nkh=nkh),
        grid=(m // bm, 2 * nkh),
        in_specs=[
            pl.BlockSpec((bm, bk), lambda i, k: (i, jnp.minimum(k, nkh - 1))),
            pl.BlockSpec((bm, bk), lambda i, k: (i, jnp.maximum(k - nkh, 0))),
            pl.BlockSpec((bk, d), lambda i, k: (k, 0)),
            pl.BlockSpec((bm, d), lambda i, k: (i, 0), pipeline_mode=pl.Buffered(1)),
            pl.BlockSpec((1, d), lambda i, k: (0, 0)),
        ],
        out_specs=pl.BlockSpec((bm, d), lambda i, k: (i, 0)),
        out_shape=jax.ShapeDtypeStruct((m, d), F32),
        compiler_params=_params("parallel", "arbitrary"),
        name="outproj",
    )(yh, yl, w, h, gain)


def _cumsum_rows(x):
    r, w = x.shape
    row = lax.broadcasted_iota(jnp.int32, (SUBLANES, w), 0)
    out = []
    carry = None
    for g in range(r // SUBLANES):
        v = x[g * SUBLANES:(g + 1) * SUBLANES]
        for dd in (1, 2, 4):
            v = v + jnp.where(row >= dd, pltpu.roll(v, dd, 0), 0.0)
        if carry is not None:
            v = v + carry
        carry = jnp.broadcast_to(v[SUBLANES - 1:SUBLANES], (SUBLANES, w))
        out.append(v)
    return jnp.concatenate(out, axis=0)


def _hgrn_gates(q_raw, f_raw, lb):
    q = q_raw * jax.nn.sigmoid(q_raw)
    sf = jax.nn.sigmoid(f_raw)
    fg = lb + (1.0 - lb) * sf
    k = (1.0 - lb) * (1.0 - sf)
    return q, k, jnp.log(fg)


def _level_reference(cum, s):
    r, w = cum.shape
    if s >= SUBLANES:
        parts = [jnp.broadcast_to(cum[p + s - 1:p + s], (2 * s, w)) for p in range(0, r, 2 * s)]
        return jnp.concatenate(parts, axis=0)
    row = lax.broadcasted_iota(jnp.int32, (SUBLANES, w), 0)
    parts = []
    for g in range(r // SUBLANES):
        v = cum[g * SUBLANES:(g + 1) * SUBLANES]
        if s == 4:
            ref = jnp.broadcast_to(v[3:4], (SUBLANES, w))
        elif s == 2:
            ref = jnp.where(row < 4, jnp.broadcast_to(v[1:2], (SUBLANES, w)),
                            jnp.broadcast_to(v[5:6], (SUBLANES, w)))
        else:
            ref = jnp.where((row & 1) == 1, pltpu.roll(v, 1, 0), v)
        parts.append(ref)
    return jnp.concatenate(parts, axis=0)


_NT = (((1,), (1,)), ((), ()))
_TN = (((0,), (0,)), ((), ()))


def _hgrn_kernel(q_ref, f_ref, i_ref, g_ref, lb_ref, gain_ref, s0_ref, o_ref, st_scr, *, hb, tb):
    @pl.when(pl.program_id(2) == 0)
    def _():
        st_scr[...] = s0_ref[...]

    w = hb * HEAD
    lb = lb_ref[...]
    gain = gain_ref[...]
    ri = lax.broadcasted_iota(jnp.int32, (CHUNK, CHUNK), 0)
    ci = lax.broadcasted_iota(jnp.int32, (CHUNK, CHUNK), 1)
    roww = lax.broadcasted_iota(jnp.int32, (CHUNK, w), 0)
    levels = (32, 16, 8, 4, 2, 1)
    diag = ri == ci
    same_block = [(ri & -(2 * s)) == (ci & -(2 * s)) for s in levels]

    def chunk(c, carry):
        r0 = pl.multiple_of(c * CHUNK, CHUNK)
        q, k, lf = _hgrn_gates(q_ref[pl.ds(r0, CHUNK), :], f_ref[pl.ds(r0, CHUNK), :], lb)
        v = i_ref[pl.ds(r0, CHUNK), :]
        graw = g_ref[pl.ds(r0, CHUNK), :]
        cum = _cumsum_rows(lf)
        last = jnp.broadcast_to(cum[CHUNK - 1:CHUNK], (CHUNK, w))
        q_in = (q * jnp.exp(cum)).astype(BF16)
        k_out = (k * jnp.exp(last - cum)).astype(BF16)
        vb = v.astype(BF16)
        qb = q.astype(BF16)
        kb = k.astype(BF16)
        qs, ks = [], []
        for s in levels:
            second = (roww & s) != 0
            ref = _level_reference(cum, s)
            e = jnp.exp(jnp.where(second, cum - ref, ref - cum))
            qk = jnp.where(second, q, k) * e
            qs.append(jnp.where(second, qk, 0.0).astype(BF16))
            ks.append(jnp.where(second, 0.0, qk).astype(BF16))
        outs = []
        for h in range(hb):
            sl = slice(h * HEAD, (h + 1) * HEAD)
            a = jnp.where(diag, lax.dot_general(qb[:, sl], kb[:, sl], _NT, preferred_element_type=F32), 0.0)
            for same, qt, kt in zip(same_block, qs, ks):
                p = lax.dot_general(qt[:, sl], kt[:, sl], _NT, preferred_element_type=F32)
                a = a + jnp.where(same, p, 0.0)
            st = st_scr[h]
            o = lax.dot_general(q_in[:, sl], st.astype(BF16), _NT, preferred_element_type=F32)
            o = o + jnp.dot(a.astype(BF16), vb[:, sl], preferred_element_type=F32)
            st_scr[h] = st * jnp.exp(last[0:1, sl]) + lax.dot_general(
                vb[:, sl], k_out[:, sl], _TN, preferred_element_type=F32)
            outs.append(o * lax.rsqrt(jnp.mean(o * o, axis=-1, keepdims=True) + EPS))
        y = jnp.concatenate(outs, axis=1) * gain * (graw * jax.nn.sigmoid(graw))
        o_ref[pl.ds(r0, CHUNK), :] = y.astype(o_ref.dtype)
        return carry

    lax.fori_loop(0, tb // CHUNK, chunk, 0, unroll=2)


def _hgrn(proj, lb, gain, s0, *, nb, hw, hb_want=4, tb_want=256):
    m = proj.shape[0]
    t = m // nb
    nh = hw // HEAD
    hb = _tile(nh, hb_want)
    w = hb * HEAD
    tb = _tile(t, tb_want)
    nt = t // tb
    ng = nh // hb

    def col(group):
        return lambda b, hg, tt: (b * nt + tt, group * ng + hg)

    return pl.pallas_call(
        functools.partial(_hgrn_kernel, hb=hb, tb=tb),
        grid=(nb, ng, nt),
        in_specs=[
            pl.BlockSpec((tb, w), col(0)),
            pl.BlockSpec((tb, w), col(1)),
            pl.BlockSpec((tb, w), col(2)),
            pl.BlockSpec((tb, w), col(3)),
            pl.BlockSpec((1, w), lambda b, hg, tt: (0, hg)),
            pl.BlockSpec((1, w), lambda b, hg, tt: (0, hg)),
            pl.BlockSpec((hb, HEAD, HEAD), lambda b, hg, tt: (hg, 0, 0)),
        ],
        out_specs=pl.BlockSpec((tb, w), lambda b, hg, tt: (b * nt + tt, hg)),
        out_shape=jax.ShapeDtypeStruct((m, hw), BF16),
        scratch_shapes=[pltpu.VMEM((hb, HEAD, HEAD), F32)],
        compiler_params=_params("parallel", "parallel", "arbitrary"),
        name="hgrn",
    )(proj, proj, proj, proj, lb, gain, s0)


def _hgrn_meta_kernel(f_ref, i_ref, lb_ref, s_ref, *, nh):
    f_raw = f_ref[...]
    _, k, lf = _hgrn_gates(f_raw, f_raw, lb_ref[...])
    cum = _cumsum_rows(lf)
    r, w = cum.shape
    last = jnp.broadcast_to(cum[r - 1:r], (r, w))
    k_out = (k * jnp.exp(last - cum)).astype(BF16)
    vb = i_ref[...].astype(BF16)
    for h in range(nh):
        sl = slice(h * HEAD, (h + 1) * HEAD)
        s_ref[h] = lax.dot_general(vb[:, sl], k_out[:, sl], _TN, preferred_element_type=F32)


def _hgrn_meta(proj_meta, lb, *, hw):
    nmeta = proj_meta.shape[0]
    nh = hw // HEAD
    return pl.pallas_call(
        functools.partial(_hgrn_meta_kernel, nh=nh),
        grid=(1,),
        in_specs=[
            pl.BlockSpec((nmeta, hw), lambda i: (0, 1)),
            pl.BlockSpec((nmeta, hw), lambda i: (0, 2)),
            pl.BlockSpec((1, hw), lambda i: (0, 0)),
        ],
        out_specs=pl.BlockSpec((nh, HEAD, HEAD), lambda i: (0, 0, 0)),
        out_shape=jax.ShapeDtypeStruct((nh, HEAD, HEAD), F32),
        compiler_params=_params("arbitrary"),
        name="hgrn_meta",
    )(proj_meta, proj_meta, lb)


def _lru_kernel(x_ref, g_ref, cw_ref, cb_ref, wa_ref, ba_ref, wx_ref, bx_ref, lam_ref, gain_ref, h0_ref, xt0_ref,
                y_ref, hl_ref, xbuf, a_scr, b_scr, hcar, *, tb, nblk, blk):
    t = pl.program_id(1)

    @pl.when(t == 0)
    def _():
        hcar[...] = h0_ref[...]
        xbuf[0:SUBLANES, :] = xt0_ref[...]

    x = x_ref[...]
    c = x.shape[1]
    xbuf[SUBLANES:SUBLANES + tb, :] = x
    cw = cw_ref[...]
    xc = (cb_ref[...] + cw[3:4] * x
          + cw[2:3] * xbuf[SUBLANES - 1:SUBLANES - 1 + tb, :]
          + cw[1:2] * xbuf[SUBLANES - 2:SUBLANES - 2 + tb, :]
          + cw[0:1] * xbuf[SUBLANES - 3:SUBLANES - 3 + tb, :])
    xbuf[0:SUBLANES, :] = x[tb - SUBLANES:tb, :]

    xcb = xc.astype(BF16)
    nlam = -lam_ref[...]
    sp = jnp.maximum(nlam, 0.0) + jnp.log(1.0 + jnp.exp(-jnp.abs(nlam)))
    for hh in range(nblk):
        sl = slice(hh * blk, (hh + 1) * blk)
        r = jax.nn.sigmoid(jnp.dot(xcb[:, sl], wa_ref[hh], preferred_element_type=F32) + ba_ref[:, sl])
        ig = jax.nn.sigmoid(jnp.dot(xcb[:, sl], wx_ref[hh], preferred_element_type=F32) + bx_ref[:, sl])
        a = jnp.exp((-RG_C) * r * sp[:, sl])
        a_scr[:, sl] = a
        b_scr[:, sl] = jnp.sqrt(1.0 - a * a) * (ig * xc[:, sl])

    row = lax.broadcasted_iota(jnp.int32, (SUBLANES, c), 0)

    def group(gi, hprev):
        r0 = pl.multiple_of(gi * SUBLANES, SUBLANES)
        av = a_scr[pl.ds(r0, SUBLANES), :]
        bv = b_scr[pl.ds(r0, SUBLANES), :]
        for dd in (1, 2, 4):
            keep = row >= dd
            bv = jnp.where(keep, av * pltpu.roll(bv, dd, 0) + bv, bv)
            av = jnp.where(keep, av * pltpu.roll(av, dd, 0), av)
        hv = av * hprev + bv
        b_scr[pl.ds(r0, SUBLANES), :] = hv
        return jnp.broadcast_to(hv[SUBLANES - 1:SUBLANES], (SUBLANES, c))

    hlast = lax.fori_loop(0, tb // SUBLANES, group, hcar[...])
    hcar[...] = hlast

    h = b_scr[...]
    gr = g_ref[...]
    gelu = 0.5 * gr * (1.0 + jnp.tanh(0.7978845608028654 * (gr + 0.044715 * (gr * gr * gr))))
    y_ref[...] = (_rms(h, gain_ref[...]) * gelu).astype(y_ref.dtype)

    @pl.when(t == pl.num_programs(1) - 1)
    def _():
        hl_ref[0] = hlast


def _lru(proj, cw, cb, wa, ba, wx, bx, lam, gain, h0, xt0, *, nb, lw, xcol, tb_want=256):
    m = proj.shape[0]
    t = m // nb
    tb = _tile(t, tb_want)
    nt = t // tb
    nblk, blk, _ = wa.shape
    vec = pl.BlockSpec((1, lw), lambda b, tt: (0, 0))
    return pl.pallas_call(
        functools.partial(_lru_kernel, tb=tb, nblk=nblk, blk=blk),
        grid=(nb, nt),
        in_specs=[
            pl.BlockSpec((tb, lw), lambda b, tt: (b * nt + tt, xcol)),
            pl.BlockSpec((tb, lw), lambda b, tt: (b * nt + tt, xcol + 1)),
            pl.BlockSpec((cw.shape[0], lw), lambda b, tt: (0, 0)),
            vec,
            pl.BlockSpec((nblk, blk, blk), lambda b, tt: (0, 0, 0)),
            vec,
            pl.BlockSpec((nblk, blk, blk), lambda b, tt: (0, 0, 0)),
            vec, vec, vec,
            pl.BlockSpec((SUBLANES, lw), lambda b, tt: (0, 0)),
            pl.BlockSpec((SUBLANES, lw), lambda b, tt: (0, 0)),
        ],
        out_specs=[
            pl.BlockSpec((tb, lw), lambda b, tt: (b * nt + tt, 0)),
            pl.BlockSpec((1, SUBLANES, lw), lambda b, tt: (b, 0, 0)),
        ],
        out_shape=[
            jax.ShapeDtypeStruct((m, lw), BF16),
            jax.ShapeDtypeStruct((nb, SUBLANES, lw), F32),
        ],
        scratch_shapes=[
            pltpu.VMEM((SUBLANES + tb, lw), F32),
            pltpu.VMEM((tb, lw), F32),
            pltpu.VMEM((tb, lw), F32),
            pltpu.VMEM((SUBLANES, lw), F32),
        ],
        compiler_params=_params("parallel", "arbitrary"),
        name="lru",
    )(proj, proj, cw, cb, wa, ba, wx, bx, lam, gain, h0, xt0)


def kernel(x, meta_tokens, ffn1_pre_norm, ffn1_w_gate, ffn1_w_up, ffn1_w_down, ffn1_post_norm, mix_pre_norm, w_in, hgrn_lb_logits, hgrn_out_norm, lru_conv_w, lru_conv_b, lru_w_a, lru_b_a, lru_w_x, lru_b_x, lru_lambda, lru_out_norm, w_out, mix_post_norm, ffn2_pre_norm, ffn2_w_gate, ffn2_w_up, ffn2_w_down, ffn2_post_norm):
    nb, seq, d = x.shape
    hw = hgrn_out_norm.shape[1]
    lw = lru_out_norm.shape[1]
    assert ffn1_pre_norm.shape[0] == 1, "one layer"
    assert hw == lw and w_in.shape[2] == 4 * hw + 2 * lw and hw % HEAD == 0
    assert seq % CHUNK == 0 and meta_tokens.shape[0] % SUBLANES == 0

    def vec(p):
        return p[0].reshape(1, -1).astype(F32)

    lb = jnp.cumsum(jax.nn.softmax(hgrn_lb_logits.astype(F32), axis=0), axis=0)[0].reshape(1, hw)
    lru_p = (lru_conv_w[0].astype(F32), vec(lru_conv_b), lru_w_a[0].astype(BF16), vec(lru_b_a),
             lru_w_x[0].astype(BF16), vec(lru_b_x), vec(lru_lambda), vec(lru_out_norm))
    xcol = (4 * hw) // lw

    hm, ffn1_w, _ = _ffn(meta_tokens.astype(F32), vec(ffn1_pre_norm), ffn1_w_gate[0], ffn1_w_up[0],
                         ffn1_w_down[0], vec(ffn1_post_norm))
    pm, w_in_b, _ = _inproj(hm, vec(mix_pre_norm), w_in[0], bn_want=512)
    s_meta = _hgrn_meta(pm, lb, hw=hw)
    zeros8 = jnp.zeros((SUBLANES, lw), F32)
    _, h_meta = _lru(pm, *lru_p, zeros8, zeros8, nb=1, lw=lw, xcol=xcol)
    xt_meta = pm[pm.shape[0] - SUBLANES:, 4 * hw:4 * hw + lw]

    h = x.reshape(nb * seq, d).astype(F32)
    h, _, ffn2_w = _ffn(h, vec(ffn1_pre_norm), *ffn1_w, vec(ffn1_post_norm),
                        side=[(ffn2_w_gate[0], False), (ffn2_w_up[0], False), (ffn2_w_down[0], True)])
    proj, _, w_out_b = _inproj(h, vec(mix_pre_norm), w_in_b, side=w_out[0])
    y_h = _hgrn(proj, lb, vec(hgrn_out_norm), s_meta, nb=nb, hw=hw)
    y_l, _ = _lru(proj, *lru_p, h_meta[0], xt_meta, nb=nb, lw=lw, xcol=xcol)
    h = _outproj(y_h, y_l, w_out_b, h, vec(mix_post_norm))
    h, _, _ = _ffn(h, vec(ffn2_pre_norm), *ffn2_w, vec(ffn2_post_norm))
    return h.reshape(nb, seq, d).astype(x.dtype)
```

```python
import functools

import jax
import jax.numpy as jnp
from jax import lax
from jax.experimental import pallas as pl
from jax.experimental.pallas import tpu as pltpu

F32 = jnp.float32
BF16 = jnp.bfloat16

EPS = 1e-6
RG_C = 8.0
HEAD = 128
CHUNK = 64
SUBLANES = 8
BF16_ROWS = 16
V7X_VMEM_LIMIT_BYTES = 60 * 1024 * 1024


def _tile(n, want):
    t = min(n, want)
    while n % t:
        t -= 1
    return t


def _rms(x, gain):
    return x * lax.rsqrt(jnp.mean(x * x, axis=-1, keepdims=True) + EPS) * gain


ROW_CHUNK = 256


def _prenorm_rows(h_ref, gain_ref, u_ref):
    step = _tile(h_ref.shape[0], ROW_CHUNK)
    for r0 in range(0, h_ref.shape[0], step):
        u_ref[r0:r0 + step, :] = _rms(h_ref[r0:r0 + step, :], gain_ref[...]).astype(u_ref.dtype)


def _postnorm_rows(h_ref, gain_ref, o_ref, scale):
    step = _tile(o_ref.shape[0], ROW_CHUNK)
    for r0 in range(0, o_ref.shape[0], step):
        o_ref[r0:r0 + step, :] = h_ref[r0:r0 + step, :] + scale * _rms(o_ref[r0:r0 + step, :], gain_ref[...])


def _params(*sem):
    return pltpu.CompilerParams(dimension_semantics=sem, vmem_limit_bytes=V7X_VMEM_LIMIT_BYTES)


def _side_specs(side, n_steps, linear):
    specs, nblks = [], []
    for arr in side:
        rows, cols = arr.shape
        assert rows % BF16_ROWS == 0
        nblk = max(k for k in range(1, n_steps + 1) if (rows // BF16_ROWS) % k == 0)
        specs.append(pl.BlockSpec((rows // nblk, cols),
                                  lambda *ids, _n=nblk: (jnp.minimum(linear(*ids), _n - 1), 0)))
        nblks.append(nblk)
    return specs, nblks


def _convert_side(step, side_in, side_out, nblks):
    for src, dst, nblk in zip(side_in, side_out, nblks):
        @pl.when(step < nblk)
        def _():
            dst[...] = src[...].astype(dst.dtype)


def _side_shapes(side):
    return [jax.ShapeDtypeStruct(arr.shape, BF16) for arr in side]


def _swiglu_accumulate(u, wg, wu, wd_ref, rows, o_ref, col_chunk, assign):
    g = jnp.dot(u, wg, preferred_element_type=F32)
    v = jnp.dot(u, wu, preferred_element_type=F32)
    a = (g * jax.nn.sigmoid(g) * v).astype(BF16)
    for c0 in range(0, o_ref.shape[1], col_chunk):
        part = jnp.dot(a, wd_ref[rows, c0:c0 + col_chunk], preferred_element_type=F32)
        if assign:
            o_ref[:, c0:c0 + col_chunk] = part
        else:
            o_ref[:, c0:c0 + col_chunk] += part


def _ffn_kernel(h_ref, gpre_ref, wg_ref, wu_ref, wd_ref, gpost_ref, o_ref, u_scr, *, first, col_chunk):
    j = pl.program_id(1)
    win = wg_ref.shape[1]
    nf = first * (win // 2)

    @pl.when(j == 0)
    def _():
        _prenorm_rows(h_ref, gpre_ref, u_scr)
        _swiglu_accumulate(u_scr[...], wg_ref[:, 0:nf], wu_ref[:, 0:nf], wd_ref, slice(0, nf), o_ref,
                           col_chunk, True)

    @pl.when(j > 0)
    def _():
        _swiglu_accumulate(u_scr[...], wg_ref[...], wu_ref[...], wd_ref, slice(0, win), o_ref, col_chunk, False)

    @pl.when(j == pl.num_programs(1) - 1)
    def _():
        _postnorm_rows(h_ref, gpost_ref, o_ref, 0.5)


def _ffn(h, gpre, wg, wu, wd, gpost, *, bm_want=512, bf_want=256):
    m, d = h.shape
    f = wg.shape[1]
    bm = _tile(m, bm_want)
    bf = _tile(f, bf_want)
    nt = f // bf
    assert nt >= 2 and wg.dtype == BF16
    first = 1 if nt % 2 else 2
    nj = 1 + (nt - first) // 2

    def off(j):
        return jnp.maximum(first + 2 * (j - 1), 0) * bf

    return pl.pallas_call(
        functools.partial(_ffn_kernel, first=first, col_chunk=_tile(d, 512)),
        grid=(m // bm, nj),
        in_specs=[
            pl.BlockSpec((bm, d), lambda i, j: (i, 0), pipeline_mode=pl.Buffered(1)),
            pl.BlockSpec((1, d), lambda i, j: (0, 0)),
            pl.BlockSpec((pl.Element(d), pl.Element(2 * bf)), lambda i, j: (0, off(j))),
            pl.BlockSpec((pl.Element(d), pl.Element(2 * bf)), lambda i, j: (0, off(j))),
            pl.BlockSpec((pl.Element(2 * bf), pl.Element(d)), lambda i, j: (off(j), 0)),
            pl.BlockSpec((1, d), lambda i, j: (0, 0)),
        ],
        out_specs=pl.BlockSpec((bm, d), lambda i, j: (i, 0), pipeline_mode=pl.Buffered(1)),
        out_shape=jax.ShapeDtypeStruct((m, d), F32),
        scratch_shapes=[pltpu.VMEM((bm, d), BF16)],
        compiler_params=_params("parallel", "arbitrary"),
        name="ffn",
    )(h, gpre, wg, wu, wd, gpost)


def _ffn_convert_kernel(h_ref, gpre_ref, wg_ref, wu_ref, wd_ref, gpost_ref, o_ref, wgb_ref, wub_ref, wdb_ref, u_scr,
                        *, col_chunk):
    j = pl.program_id(0)

    @pl.when(j == 0)
    def _():
        _prenorm_rows(h_ref, gpre_ref, u_scr)
        o_ref[...] = jnp.zeros_like(o_ref)

    wgb_ref[...] = wg_ref[...].astype(BF16)
    wub_ref[...] = wu_ref[...].astype(BF16)
    wdb_ref[...] = wd_ref[...].astype(BF16)
    _swiglu_accumulate(u_scr[...], wgb_ref[...], wub_ref[...], wdb_ref, slice(None), o_ref, col_chunk, False)

    @pl.when(j == pl.num_programs(0) - 1)
    def _():
        _postnorm_rows(h_ref, gpost_ref, o_ref, 0.5)


def _ffn_convert(h, gpre, wg, wu, wd, gpost, *, bf_want=256):
    m, d = h.shape
    f = wg.shape[1]
    bf = _tile(f, bf_want)
    cols = pl.BlockSpec((d, bf), lambda j: (0, j))
    rows = pl.BlockSpec((bf, d), lambda j: (j, 0))
    full = pl.BlockSpec((m, d), lambda j: (0, 0))
    vec = pl.BlockSpec((1, d), lambda j: (0, 0))
    res = pl.pallas_call(
        functools.partial(_ffn_convert_kernel, col_chunk=_tile(d, 512)),
        grid=(f // bf,),
        in_specs=[full, vec, cols, cols, rows, vec],
        out_specs=[full, cols, cols, rows],
        out_shape=[jax.ShapeDtypeStruct((m, d), F32)] + [jax.ShapeDtypeStruct(w.shape, BF16) for w in (wg, wu, wd)],
        scratch_shapes=[pltpu.VMEM((m, d), BF16)],
        compiler_params=_params("arbitrary"),
        name="ffn_convert",
    )(h, gpre, wg, wu, wd, gpost)
    return res[0], tuple(res[1:])


def _inproj_kernel(*refs, emit):
    h_ref, g_ref, w_ref, o_ref = refs[:4]
    u_scr = refs[-1]

    @pl.when(pl.program_id(1) == 0)
    def _():
        _prenorm_rows(h_ref, g_ref, u_scr)

    if emit:
        wb_ref = refs[4]
        wb_ref[...] = w_ref[...].astype(BF16)
        w_ref = wb_ref
    o_ref[...] = jnp.dot(u_scr[...], w_ref[...], preferred_element_type=F32)


def _inproj(h, gain, w, *, bm_want=1024, bn_want=512):
    m, d = h.shape
    n = w.shape[1]
    bm = _tile(m, bm_want)
    bn = _tile(n, bn_want)
    emit = w.dtype != BF16
    out_specs = [pl.BlockSpec((bm, bn), lambda i, j: (i, j))]
    out_shape = [jax.ShapeDtypeStruct((m, n), F32)]
    if emit:
        assert m == bm, "weights are converted on a single pass over the column tiles"
        out_specs.append(pl.BlockSpec((d, bn), lambda i, j: (0, j)))
        out_shape.append(jax.ShapeDtypeStruct(w.shape, BF16))
    res = pl.pallas_call(
        functools.partial(_inproj_kernel, emit=emit),
        grid=(m // bm, n // bn),
        in_specs=[
            pl.BlockSpec((bm, d), lambda i, j: (i, 0), pipeline_mode=pl.Buffered(1)),
            pl.BlockSpec((1, d), lambda i, j: (0, 0)),
            pl.BlockSpec((d, bn), lambda i, j: (0, j)),
        ],
        out_specs=out_specs,
        out_shape=out_shape,
        scratch_shapes=[pltpu.VMEM((bm, d), BF16)],
        compiler_params=_params("parallel", "arbitrary"),
        name="inproj_convert" if emit else "inproj",
    )(h, gain, w)
    return res[0], (res[1] if emit else w)


def _outproj_kernel(yh_ref, yl_ref, w_ref, h_ref, g_ref, o_ref, *, col_chunk):
    kh = yh_ref.shape[1]
    yh = yh_ref[...]
    yl = yl_ref[...]
    for c0 in range(0, o_ref.shape[1], col_chunk):
        o_ref[:, c0:c0 + col_chunk] = (
            jnp.dot(yh, w_ref[0:kh, c0:c0 + col_chunk], preferred_element_type=F32)
            + jnp.dot(yl, w_ref[kh:2 * kh, c0:c0 + col_chunk], preferred_element_type=F32))
    _postnorm_rows(h_ref, g_ref, o_ref, 1.0)


def _outproj(yh, yl, w, h, gain, *, bm_want=128):
    m, kh = yh.shape
    d = w.shape[1]
    bm = _tile(m, bm_want)
    return pl.pallas_call(
        functools.partial(_outproj_kernel, col_chunk=_tile(d, 1024)),
        grid=(m // bm,),
        in_specs=[
            pl.BlockSpec((bm, kh), lambda i: (i, 0)),
            pl.BlockSpec((bm, kh), lambda i: (i, 0)),
            pl.BlockSpec((2 * kh, d), lambda i: (0, 0), pipeline_mode=pl.Buffered(1)),
            pl.BlockSpec((bm, d), lambda i: (i, 0)),
            pl.BlockSpec((1, d), lambda i: (0, 0)),
        ],
        out_specs=pl.BlockSpec((bm, d), lambda i: (i, 0)),
        out_shape=jax.ShapeDtypeStruct((m, d), F32),
        compiler_params=_params("parallel"),
        name="outproj",
    )(yh, yl, w, h, gain)


def _cumsum_rows(x):
    r, w = x.shape
    row = lax.broadcasted_iota(jnp.int32, (SUBLANES, w), 0)
    out = []
    carry = None
    for g in range(r // SUBLANES):
        v = x[g * SUBLANES:(g + 1) * SUBLANES]
        for dd in (1, 2, 4):
            v = v + jnp.where(row >= dd, pltpu.roll(v, dd, 0), 0.0)
        if carry is not None:
            v = v + carry
        carry = jnp.broadcast_to(v[SUBLANES - 1:SUBLANES], (SUBLANES, w))
        out.append(v)
    return jnp.concatenate(out, axis=0)


def _hgrn_gates(q_raw, f_raw, lb):
    q = q_raw * jax.nn.sigmoid(q_raw)
    sf = jax.nn.sigmoid(f_raw)
    fg = lb + (1.0 - lb) * sf
    k = (1.0 - lb) * (1.0 - sf)
    return q, k, jnp.log(fg)


def _level_reference(cum, s):
    r, w = cum.shape
    if s >= SUBLANES:
        parts = [jnp.broadcast_to(cum[p + s - 1:p + s], (2 * s, w)) for p in range(0, r, 2 * s)]
        return jnp.concatenate(parts, axis=0)
    row = lax.broadcasted_iota(jnp.int32, (SUBLANES, w), 0)
    parts = []
    for g in range(r // SUBLANES):
        v = cum[g * SUBLANES:(g + 1) * SUBLANES]
        if s == 4:
            ref = jnp.broadcast_to(v[3:4], (SUBLANES, w))
        elif s == 2:
            ref = jnp.where(row < 4, jnp.broadcast_to(v[1:2], (SUBLANES, w)),
                            jnp.broadcast_to(v[5:6], (SUBLANES, w)))
        else:
            ref = jnp.where((row & 1) == 1, pltpu.roll(v, 1, 0), v)
        parts.append(ref)
    return jnp.concatenate(parts, axis=0)


_NT = (((1,), (1,)), ((), ()))
_TN = (((0,), (0,)), ((), ()))


def _hgrn_kernel(*refs, hb, tb, nblks):
    q_ref, f_ref, i_ref, g_ref, lb_ref, gain_ref, s0_ref = refs[:7]
    ns = len(nblks)
    side_in = refs[7:7 + ns]
    o_ref = refs[7 + ns]
    side_out = refs[8 + ns:8 + 2 * ns]
    st_scr = refs[-1]

    @pl.when(pl.program_id(2) == 0)
    def _():
        st_scr[...] = s0_ref[...]

    step = (pl.program_id(0) * pl.num_programs(1) + pl.program_id(1)) * pl.num_programs(2) + pl.program_id(2)
    _convert_side(step, side_in, side_out, nblks)

    w = hb * HEAD
    lb = lb_ref[...]
    gain = gain_ref[...]
    ri = lax.broadcasted_iota(jnp.int32, (CHUNK, CHUNK), 0)
    ci = lax.broadcasted_iota(jnp.int32, (CHUNK, CHUNK), 1)
    roww = lax.broadcasted_iota(jnp.int32, (CHUNK, w), 0)
    levels = (32, 16, 8, 4, 2, 1)
    diag = ri == ci
    same_block = [(ri & -(2 * s)) == (ci & -(2 * s)) for s in levels]

    def chunk(c, carry):
        r0 = pl.multiple_of(c * CHUNK, CHUNK)
        q, k, lf = _hgrn_gates(q_ref[pl.ds(r0, CHUNK), :], f_ref[pl.ds(r0, CHUNK), :], lb)
        v = i_ref[pl.ds(r0, CHUNK), :]
        graw = g_ref[pl.ds(r0, CHUNK), :]
        cum = _cumsum_rows(lf)
        last = jnp.broadcast_to(cum[CHUNK - 1:CHUNK], (CHUNK, w))
        q_in = (q * jnp.exp(cum)).astype(BF16)
        k_out = (k * jnp.exp(last - cum)).astype(BF16)
        vb = v.astype(BF16)
        qb = q.astype(BF16)
        kb = k.astype(BF16)
        qs, ks = [], []
        for s in levels:
            second = (roww & s) != 0
            ref = _level_reference(cum, s)
            e = jnp.exp(jnp.where(second, cum - ref, ref - cum))
            qk = jnp.where(second, q, k) * e
            qs.append(jnp.where(second, qk, 0.0).astype(BF16))
            ks.append(jnp.where(second, 0.0, qk).astype(BF16))
        outs = []
        for h in range(hb):
            sl = slice(h * HEAD, (h + 1) * HEAD)
            a = jnp.where(diag, lax.dot_general(qb[:, sl], kb[:, sl], _NT, preferred_element_type=F32), 0.0)
            for same, qt, kt in zip(same_block, qs, ks):
                p = lax.dot_general(qt[:, sl], kt[:, sl], _NT, preferred_element_type=F32)
                a = a + jnp.where(same, p, 0.0)
            st = st_scr[h]
            o = lax.dot_general(q_in[:, sl], st.astype(BF16), _NT, preferred_element_type=F32)
            o = o + jnp.dot(a.astype(BF16), vb[:, sl], preferred_element_type=F32)
            st_scr[h] = st * jnp.exp(last[0:1, sl]) + lax.dot_general(
                vb[:, sl], k_out[:, sl], _TN, preferred_element_type=F32)
            outs.append(o * lax.rsqrt(jnp.mean(o * o, axis=-1, keepdims=True) + EPS))
        y = jnp.concatenate(outs, axis=1) * gain * (graw * jax.nn.sigmoid(graw))
        o_ref[pl.ds(r0, CHUNK), :] = y.astype(o_ref.dtype)
        return carry

    lax.fori_loop(0, tb // CHUNK, chunk, 0, unroll=2)


def _hgrn(proj, lb, gain, s0, side=(), *, nb, hw, hb_want=4, tb_want=256):
    m = proj.shape[0]
    t = m // nb
    nh = hw // HEAD
    hb = _tile(nh, hb_want)
    w = hb * HEAD
    tb = _tile(t, tb_want)
    nt = t // tb
    ng = nh // hb

    def col(group):
        return lambda b, hg, tt: (b * nt + tt, group * ng + hg)

    side_specs, nblks = _side_specs(side, nb * ng * nt, lambda b, hg, tt: (b * ng + hg) * nt + tt)
    res = pl.pallas_call(
        functools.partial(_hgrn_kernel, hb=hb, tb=tb, nblks=tuple(nblks)),
        grid=(nb, ng, nt),
        in_specs=[
            pl.BlockSpec((tb, w), col(0)),
            pl.BlockSpec((tb, w), col(1)),
            pl.BlockSpec((tb, w), col(2)),
            pl.BlockSpec((tb, w), col(3)),
            pl.BlockSpec((1, w), lambda b, hg, tt: (0, hg)),
            pl.BlockSpec((1, w), lambda b, hg, tt: (0, hg)),
            pl.BlockSpec((hb, HEAD, HEAD), lambda b, hg, tt: (hg, 0, 0)),
        ] + side_specs,
        out_specs=[pl.BlockSpec((tb, w), lambda b, hg, tt: (b * nt + tt, hg))] + side_specs,
        out_shape=[jax.ShapeDtypeStruct((m, hw), BF16)] + _side_shapes(side),
        scratch_shapes=[pltpu.VMEM((hb, HEAD, HEAD), F32)],
        compiler_params=_params("arbitrary", "arbitrary", "arbitrary"),
        name="hgrn",
    )(proj, proj, proj, proj, lb, gain, s0, *side)
    return res[0], tuple(res[1:])


def _hgrn_meta_kernel(f_ref, i_ref, lb_ref, s_ref, *, nh):
    f_raw = f_ref[...]
    _, k, lf = _hgrn_gates(f_raw, f_raw, lb_ref[...])
    cum = _cumsum_rows(lf)
    r, w = cum.shape
    last = jnp.broadcast_to(cum[r - 1:r], (r, w))
    k_out = (k * jnp.exp(last - cum)).astype(BF16)
    vb = i_ref[...].astype(BF16)
    for h in range(nh):
        sl = slice(h * HEAD, (h + 1) * HEAD)
        s_ref[h] = lax.dot_general(vb[:, sl], k_out[:, sl], _TN, preferred_element_type=F32)


def _hgrn_meta(proj_meta, lb, *, hw):
    nmeta = proj_meta.shape[0]
    nh = hw // HEAD
    return pl.pallas_call(
        functools.partial(_hgrn_meta_kernel, nh=nh),
        grid=(1,),
        in_specs=[
            pl.BlockSpec((nmeta, hw), lambda i: (0, 1)),
            pl.BlockSpec((nmeta, hw), lambda i: (0, 2)),
            pl.BlockSpec((1, hw), lambda i: (0, 0)),
        ],
        out_specs=pl.BlockSpec((nh, HEAD, HEAD), lambda i: (0, 0, 0)),
        out_shape=jax.ShapeDtypeStruct((nh, HEAD, HEAD), F32),
        compiler_params=_params("arbitrary"),
        name="hgrn_meta",
    )(proj_meta, proj_meta, lb)


def _lru_kernel(*refs, tb, nblk, blk, nblks):
    (x_ref, g_ref, cw_ref, cb_ref, wa_ref, ba_ref, wx_ref, bx_ref, lam_ref, gain_ref, h0_ref, xt0_ref) = refs[:12]
    ns = len(nblks)
    side_in = refs[12:12 + ns]
    y_ref, hl_ref = refs[12 + ns:14 + ns]
    side_out = refs[14 + ns:14 + 2 * ns]
    xbuf, a_scr, b_scr, hcar = refs[-4:]
    t = pl.program_id(1)

    @pl.when(t == 0)
    def _():
        hcar[...] = h0_ref[...]
        xbuf[0:SUBLANES, :] = xt0_ref[...]

    _convert_side(pl.program_id(0) * pl.num_programs(1) + t, side_in, side_out, nblks)

    x = x_ref[...]
    c = x.shape[1]
    xbuf[SUBLANES:SUBLANES + tb, :] = x
    cw = cw_ref[...]
    xc = (cb_ref[...] + cw[3:4] * x
          + cw[2:3] * xbuf[SUBLANES - 1:SUBLANES - 1 + tb, :]
          + cw[1:2] * xbuf[SUBLANES - 2:SUBLANES - 2 + tb, :]
          + cw[0:1] * xbuf[SUBLANES - 3:SUBLANES - 3 + tb, :])
    xbuf[0:SUBLANES, :] = x[tb - SUBLANES:tb, :]

    xcb = xc.astype(BF16)
    nlam = -lam_ref[...]
    sp = jnp.maximum(nlam, 0.0) + jnp.log(1.0 + jnp.exp(-jnp.abs(nlam)))
    for hh in range(nblk):
        sl = slice(hh * blk, (hh + 1) * blk)
        r = jax.nn.sigmoid(jnp.dot(xcb[:, sl], wa_ref[hh], preferred_element_type=F32) + ba_ref[:, sl])
        ig = jax.nn.sigmoid(jnp.dot(xcb[:, sl], wx_ref[hh], preferred_element_type=F32) + bx_ref[:, sl])
        a = jnp.exp((-RG_C) * r * sp[:, sl])
        a_scr[:, sl] = a
        b_scr[:, sl] = jnp.sqrt(1.0 - a * a) * (ig * xc[:, sl])

    row = lax.broadcasted_iota(jnp.int32, (SUBLANES, c), 0)

    def group(gi, hprev):
        r0 = pl.multiple_of(gi * SUBLANES, SUBLANES)
        av = a_scr[pl.ds(r0, SUBLANES), :]
        bv = b_scr[pl.ds(r0, SUBLANES), :]
        for dd in (1, 2, 4):
            keep = row >= dd
            bv = jnp.where(keep, av * pltpu.roll(bv, dd, 0) + bv, bv)
            av = jnp.where(keep, av * pltpu.roll(av, dd, 0), av)
        hv = av * hprev + bv
        b_scr[pl.ds(r0, SUBLANES), :] = hv
        return jnp.broadcast_to(hv[SUBLANES - 1:SUBLANES], (SUBLANES, c))

    hlast = lax.fori_loop(0, tb // SUBLANES, group, hcar[...])
    hcar[...] = hlast

    h = b_scr[...]
    gr = g_ref[...]
    gelu = 0.5 * gr * (1.0 + jnp.tanh(0.7978845608028654 * (gr + 0.044715 * (gr * gr * gr))))
    y_ref[...] = (_rms(h, gain_ref[...]) * gelu).astype(y_ref.dtype)

    @pl.when(t == pl.num_programs(1) - 1)
    def _():
        hl_ref[0] = hlast


def _lru(proj, cw, cb, wa, ba, wx, bx, lam, gain, h0, xt0, side=(), *, nb, lw, xcol, tb_want=256):
    m = proj.shape[0]
    t = m // nb
    tb = _tile(t, tb_want)
    nt = t // tb
    nblk, blk, _ = wa.shape
    vec = pl.BlockSpec((1, lw), lambda b, tt: (0, 0))
    side_specs, nblks = _side_specs(side, nb * nt, lambda b, tt: b * nt + tt)
    res = pl.pallas_call(
        functools.partial(_lru_kernel, tb=tb, nblk=nblk, blk=blk, nblks=tuple(nblks)),
        grid=(nb, nt),
        in_specs=[
            pl.BlockSpec((tb, lw), lambda b, tt: (b * nt + tt, xcol)),
            pl.BlockSpec((tb, lw), lambda b, tt: (b * nt + tt, xcol + 1)),
            pl.BlockSpec((cw.shape[0], lw), lambda b, tt: (0, 0)),
            vec,
            pl.BlockSpec((nblk, blk, blk), lambda b, tt: (0, 0, 0)),
            vec,
            pl.BlockSpec((nblk, blk, blk), lambda b, tt: (0, 0, 0)),
            vec, vec, vec,
            pl.BlockSpec((SUBLANES, lw), lambda b, tt: (0, 0)),
            pl.BlockSpec((SUBLANES, lw), lambda b, tt: (0, 0)),
        ] + side_specs,
        out_specs=[
            pl.BlockSpec((tb, lw), lambda b, tt: (b * nt + tt, 0)),
            pl.BlockSpec((1, SUBLANES, lw), lambda b, tt: (b, 0, 0)),
        ] + side_specs,
        out_shape=[
            jax.ShapeDtypeStruct((m, lw), BF16),
            jax.ShapeDtypeStruct((nb, SUBLANES, lw), F32),
        ] + _side_shapes(side),
        scratch_shapes=[
            pltpu.VMEM((SUBLANES + tb, lw), F32),
            pltpu.VMEM((tb, lw), F32),
            pltpu.VMEM((tb, lw), F32),
            pltpu.VMEM((SUBLANES, lw), F32),
        ],
        compiler_params=_params("arbitrary", "arbitrary"),
        name="lru",
    )(proj, proj, cw, cb, wa, ba, wx, bx, lam, gain, h0, xt0, *side)
    return res[0], res[1], tuple(res[2:])


def kernel(x, meta_tokens, ffn1_pre_norm, ffn1_w_gate, ffn1_w_up, ffn1_w_down, ffn1_post_norm, mix_pre_norm, w_in, hgrn_lb_logits, hgrn_out_norm, lru_conv_w, lru_conv_b, lru_w_a, lru_b_a, lru_w_x, lru_b_x, lru_lambda, lru_out_norm, w_out, mix_post_norm, ffn2_pre_norm, ffn2_w_gate, ffn2_w_up, ffn2_w_down, ffn2_post_norm):
    nb, seq, d = x.shape
    hw = hgrn_out_norm.shape[1]
    lw = lru_out_norm.shape[1]
    assert ffn1_pre_norm.shape[0] == 1, "one layer"
    assert hw == lw and w_in.shape[2] == 4 * hw + 2 * lw and hw % HEAD == 0
    assert seq % CHUNK == 0 and meta_tokens.shape[0] % SUBLANES == 0

    def vec(p):
        return p[0].reshape(1, -1).astype(F32)

    lb = jnp.cumsum(jax.nn.softmax(hgrn_lb_logits.astype(F32), axis=0), axis=0)[0].reshape(1, hw)
    lru_p = (lru_conv_w[0].astype(F32), vec(lru_conv_b), lru_w_a[0].astype(BF16), vec(lru_b_a),
             lru_w_x[0].astype(BF16), vec(lru_b_x), vec(lru_lambda), vec(lru_out_norm))
    xcol = (4 * hw) // lw

    hm, ffn1_w = _ffn_convert(meta_tokens.astype(F32), vec(ffn1_pre_norm), ffn1_w_gate[0], ffn1_w_up[0],
                              ffn1_w_down[0], vec(ffn1_post_norm))
    pm, w_in_b = _inproj(hm, vec(mix_pre_norm), w_in[0], bn_want=512)
    s_meta = _hgrn_meta(pm, lb, hw=hw)
    zeros8 = jnp.zeros((SUBLANES, lw), F32)
    _, h_meta, _ = _lru(pm, *lru_p, zeros8, zeros8, nb=1, lw=lw, xcol=xcol)
    xt_meta = pm[pm.shape[0] - SUBLANES:, 4 * hw:4 * hw + lw]

    h = x.reshape(nb * seq, d).astype(F32)
    h = _ffn(h, vec(ffn1_pre_norm), *ffn1_w, vec(ffn1_post_norm))
    proj, _ = _inproj(h, vec(mix_pre_norm), w_in_b)
    y_h, ffn2_w = _hgrn(proj, lb, vec(hgrn_out_norm), s_meta, (ffn2_w_gate[0], ffn2_w_up[0], ffn2_w_down[0]),
                        nb=nb, hw=hw)
    y_l, _, (w_out_b,) = _lru(proj, *lru_p, h_meta[0], xt_meta, (w_out[0],), nb=nb, lw=lw, xcol=xcol)
    h = _outproj(y_h, y_l, w_out_b, h, vec(mix_post_norm))
    h = _ffn(h, vec(ffn2_pre_norm), *ffn2_w, vec(ffn2_post_norm))
    return h.reshape(nb, seq, d).astype(x.dtype)
```

```python
import functools

import jax
import jax.numpy as jnp
from jax import lax
from jax.experimental import pallas as pl
from jax.experimental.pallas import tpu as pltpu

F32 = jnp.float32
BF16 = jnp.bfloat16

EPS = 1e-6
RG_C = 8.0
LOG2E = 1.4426950408889634
HEAD = 128
CHUNK = 64
SUBLANES = 8
BF16_ROWS = 16
V7X_VMEM_LIMIT_BYTES = 60 * 1024 * 1024


def _tile(n, want):
    t = min(n, want)
    while n % t:
        t -= 1
    return t


def _sigmoid(x):
    return 0.5 * jnp.tanh(0.5 * x) + 0.5


def _rms(x, gain):
    return x * lax.rsqrt(jnp.mean(x * x, axis=-1, keepdims=True) + EPS) * gain


ROW_CHUNK = 256


def _prenorm_rows(h_ref, gain_ref, u_ref):
    step = _tile(h_ref.shape[0], ROW_CHUNK)
    for r0 in range(0, h_ref.shape[0], step):
        u_ref[r0:r0 + step, :] = _rms(h_ref[r0:r0 + step, :], gain_ref[...]).astype(u_ref.dtype)


def _postnorm_rows(h_ref, gain_ref, o_ref, scale):
    step = _tile(o_ref.shape[0], ROW_CHUNK)
    for r0 in range(0, o_ref.shape[0], step):
        o_ref[r0:r0 + step, :] = h_ref[r0:r0 + step, :] + scale * _rms(o_ref[r0:r0 + step, :], gain_ref[...])


def _params(*sem):
    return pltpu.CompilerParams(dimension_semantics=sem, vmem_limit_bytes=V7X_VMEM_LIMIT_BYTES)


def _side_specs(side, n_steps, linear):
    specs, nblks = [], []
    for arr in side:
        rows, cols = arr.shape
        assert rows % BF16_ROWS == 0
        nblk = max(k for k in range(1, n_steps + 1) if (rows // BF16_ROWS) % k == 0)
        specs.append(pl.BlockSpec((rows // nblk, cols),
                                  lambda *ids, _n=nblk: (jnp.minimum(linear(*ids), _n - 1), 0)))
        nblks.append(nblk)
    return specs, nblks


def _convert_side(step, side_in, side_out, nblks):
    for src, dst, nblk in zip(side_in, side_out, nblks):
        @pl.when(step < nblk)
        def _():
            dst[...] = src[...].astype(dst.dtype)


def _side_shapes(side):
    return [jax.ShapeDtypeStruct(arr.shape, BF16) for arr in side]


def _swiglu_accumulate(u, wg, wu, wd_ref, rows, o_ref, col_chunk, assign):
    g = jnp.dot(u, wg, preferred_element_type=F32)
    v = jnp.dot(u, wu, preferred_element_type=F32)
    a = (g * jax.nn.sigmoid(g) * v).astype(BF16)
    for c0 in range(0, o_ref.shape[1], col_chunk):
        part = jnp.dot(a, wd_ref[rows, c0:c0 + col_chunk], preferred_element_type=F32)
        if assign:
            o_ref[:, c0:c0 + col_chunk] = part
        else:
            o_ref[:, c0:c0 + col_chunk] += part


def _ffn_kernel(h_ref, gpre_ref, wg_ref, wu_ref, wd_ref, gpost_ref, o_ref, u_scr, *, first, col_chunk):
    j = pl.program_id(1)
    win = wg_ref.shape[1]
    nf = first * (win // 2)

    @pl.when(j == 0)
    def _():
        _prenorm_rows(h_ref, gpre_ref, u_scr)
        _swiglu_accumulate(u_scr[...], wg_ref[:, 0:nf], wu_ref[:, 0:nf], wd_ref, slice(0, nf), o_ref,
                           col_chunk, True)

    @pl.when(j > 0)
    def _():
        _swiglu_accumulate(u_scr[...], wg_ref[...], wu_ref[...], wd_ref, slice(0, win), o_ref, col_chunk, False)

    @pl.when(j == pl.num_programs(1) - 1)
    def _():
        _postnorm_rows(h_ref, gpost_ref, o_ref, 0.5)


def _ffn(h, gpre, wg, wu, wd, gpost, *, bm_want=512, bf_want=256):
    m, d = h.shape
    f = wg.shape[1]
    bm = _tile(m, bm_want)
    bf = _tile(f, bf_want)
    nt = f // bf
    assert nt >= 2 and wg.dtype == BF16
    first = 1 if nt % 2 else 2
    nj = 1 + (nt - first) // 2

    def off(j):
        return jnp.maximum(first + 2 * (j - 1), 0) * bf

    return pl.pallas_call(
        functools.partial(_ffn_kernel, first=first, col_chunk=_tile(d, 512)),
        grid=(m // bm, nj),
        in_specs=[
            pl.BlockSpec((bm, d), lambda i, j: (i, 0), pipeline_mode=pl.Buffered(1)),
            pl.BlockSpec((1, d), lambda i, j: (0, 0)),
            pl.BlockSpec((pl.Element(d), pl.Element(2 * bf)), lambda i, j: (0, off(j))),
            pl.BlockSpec((pl.Element(d), pl.Element(2 * bf)), lambda i, j: (0, off(j))),
            pl.BlockSpec((pl.Element(2 * bf), pl.Element(d)), lambda i, j: (off(j), 0)),
            pl.BlockSpec((1, d), lambda i, j: (0, 0)),
        ],
        out_specs=pl.BlockSpec((bm, d), lambda i, j: (i, 0), pipeline_mode=pl.Buffered(1)),
        out_shape=jax.ShapeDtypeStruct((m, d), F32),
        scratch_shapes=[pltpu.VMEM((bm, d), BF16)],
        compiler_params=_params("parallel", "arbitrary"),
        name="ffn",
    )(h, gpre, wg, wu, wd, gpost)


def _ffn_convert_kernel(h_ref, gpre_ref, wg_ref, wu_ref, wd_ref, gpost_ref, o_ref, wgb_ref, wub_ref, wdb_ref, u_scr,
                        *, col_chunk):
    j = pl.program_id(0)

    @pl.when(j == 0)
    def _():
        _prenorm_rows(h_ref, gpre_ref, u_scr)
        o_ref[...] = jnp.zeros_like(o_ref)

    wgb_ref[...] = wg_ref[...].astype(BF16)
    wub_ref[...] = wu_ref[...].astype(BF16)
    wdb_ref[...] = wd_ref[...].astype(BF16)
    _swiglu_accumulate(u_scr[...], wgb_ref[...], wub_ref[...], wdb_ref, slice(None), o_ref, col_chunk, False)

    @pl.when(j == pl.num_programs(0) - 1)
    def _():
        _postnorm_rows(h_ref, gpost_ref, o_ref, 0.5)


def _ffn_convert(h, gpre, wg, wu, wd, gpost, *, bf_want=256):
    m, d = h.shape
    f = wg.shape[1]
    bf = _tile(f, bf_want)
    cols = pl.BlockSpec((d, bf), lambda j: (0, j))
    rows = pl.BlockSpec((bf, d), lambda j: (j, 0))
    full = pl.BlockSpec((m, d), lambda j: (0, 0))
    vec = pl.BlockSpec((1, d), lambda j: (0, 0))
    res = pl.pallas_call(
        functools.partial(_ffn_convert_kernel, col_chunk=_tile(d, 512)),
        grid=(f // bf,),
        in_specs=[full, vec, cols, cols, rows, vec],
        out_specs=[full, cols, cols, rows],
        out_shape=[jax.ShapeDtypeStruct((m, d), F32)] + [jax.ShapeDtypeStruct(w.shape, BF16) for w in (wg, wu, wd)],
        scratch_shapes=[pltpu.VMEM((m, d), BF16)],
        compiler_params=_params("arbitrary"),
        name="ffn_convert",
    )(h, gpre, wg, wu, wd, gpost)
    return res[0], tuple(res[1:])


def _inproj_kernel(*refs, emit, nblks):
    h_ref, g_ref, w_ref = refs[:3]
    ns = len(nblks)
    side_in = refs[3:3 + ns]
    o_ref = refs[3 + ns]
    side_out = refs[len(refs) - 1 - ns:len(refs) - 1]
    u_scr = refs[-1]

    @pl.when(pl.program_id(1) == 0)
    def _():
        _prenorm_rows(h_ref, g_ref, u_scr)

    _convert_side(pl.program_id(0) * pl.num_programs(1) + pl.program_id(1), side_in, side_out, nblks)
    if emit:
        wb_ref = refs[4 + ns]
        wb_ref[...] = w_ref[...].astype(BF16)
        w_ref = wb_ref
    o_ref[...] = jnp.dot(u_scr[...], w_ref[...], preferred_element_type=F32)


def _inproj(h, gain, w, side=(), *, bm_want=1024, bn_want=512):
    m, d = h.shape
    n = w.shape[1]
    bm = _tile(m, bm_want)
    bn = _tile(n, bn_want)
    nj = n // bn
    emit = w.dtype != BF16
    out_specs = [pl.BlockSpec((bm, bn), lambda i, j: (i, j))]
    out_shape = [jax.ShapeDtypeStruct((m, n), F32)]
    if emit:
        assert m == bm, "weights are converted on a single pass over the column tiles"
        out_specs.append(pl.BlockSpec((d, bn), lambda i, j: (0, j)))
        out_shape.append(jax.ShapeDtypeStruct(w.shape, BF16))
    side_specs, nblks = _side_specs(side, (m // bm) * nj, lambda i, j: i * nj + j)
    res = pl.pallas_call(
        functools.partial(_inproj_kernel, emit=emit, nblks=tuple(nblks)),
        grid=(m // bm, nj),
        in_specs=[
            pl.BlockSpec((bm, d), lambda i, j: (i, 0), pipeline_mode=pl.Buffered(1)),
            pl.BlockSpec((1, d), lambda i, j: (0, 0)),
            pl.BlockSpec((d, bn), lambda i, j: (0, j)),
        ] + side_specs,
        out_specs=out_specs + side_specs,
        out_shape=out_shape + _side_shapes(side),
        scratch_shapes=[pltpu.VMEM((bm, d), BF16)],
        compiler_params=_params("arbitrary", "arbitrary"),
        name="inproj_convert" if emit else "inproj",
    )(h, gain, w, *side)
    return res[0], (res[1] if emit else w), tuple(res[len(res) - len(side):]) if side else ()


def _outproj_kernel(yh_ref, yl_ref, w_ref, h_ref, g_ref, o_ref, *, col_chunk):
    kh = yh_ref.shape[1]
    yh = yh_ref[...]
    yl = yl_ref[...]
    for c0 in range(0, o_ref.shape[1], col_chunk):
        o_ref[:, c0:c0 + col_chunk] = (
            jnp.dot(yh, w_ref[0:kh, c0:c0 + col_chunk], preferred_element_type=F32)
            + jnp.dot(yl, w_ref[kh:2 * kh, c0:c0 + col_chunk], preferred_element_type=F32))
    _postnorm_rows(h_ref, g_ref, o_ref, 1.0)


def _outproj(yh, yl, w, h, gain, *, bm_want=128):
    m, kh = yh.shape
    d = w.shape[1]
    bm = _tile(m, bm_want)
    return pl.pallas_call(
        functools.partial(_outproj_kernel, col_chunk=_tile(d, 1024)),
        grid=(m // bm,),
        in_specs=[
            pl.BlockSpec((bm, kh), lambda i: (i, 0)),
            pl.BlockSpec((bm, kh), lambda i: (i, 0)),
            pl.BlockSpec((2 * kh, d), lambda i: (0, 0), pipeline_mode=pl.Buffered(1)),
            pl.BlockSpec((bm, d), lambda i: (i, 0)),
            pl.BlockSpec((1, d), lambda i: (0, 0)),
        ],
        out_specs=pl.BlockSpec((bm, d), lambda i: (i, 0)),
        out_shape=jax.ShapeDtypeStruct((m, d), F32),
        compiler_params=_params("parallel"),
        name="outproj",
    )(yh, yl, w, h, gain)


def _cumsum_rows(x):
    r, w = x.shape
    row = lax.broadcasted_iota(jnp.int32, (SUBLANES, w), 0)
    out = []
    carry = None
    for g in range(r // SUBLANES):
        v = x[g * SUBLANES:(g + 1) * SUBLANES]
        for dd in (1, 2, 4):
            v = v + jnp.where(row >= dd, pltpu.roll(v, dd, 0), 0.0)
        if carry is not None:
            v = v + carry
        carry = jnp.broadcast_to(v[SUBLANES - 1:SUBLANES], (SUBLANES, w))
        out.append(v)
    return jnp.concatenate(out, axis=0)


def _hgrn_gates(q_raw, f_raw, lb):
    q = q_raw * _sigmoid(q_raw)
    sf = _sigmoid(f_raw)
    fg = lb + (1.0 - lb) * sf
    k = (1.0 - lb) * (1.0 - sf)
    return q, k, jnp.log(fg) * LOG2E, fg


def _level_reference(cum, s):
    r, w = cum.shape
    if s >= SUBLANES:
        parts = [jnp.broadcast_to(cum[p + s - 1:p + s], (2 * s, w)) for p in range(0, r, 2 * s)]
        return jnp.concatenate(parts, axis=0)
    row = lax.broadcasted_iota(jnp.int32, (SUBLANES, w), 0)
    parts = []
    for g in range(r // SUBLANES):
        v = cum[g * SUBLANES:(g + 1) * SUBLANES]
        if s == 4:
            ref = jnp.broadcast_to(v[3:4], (SUBLANES, w))
        elif s == 2:
            ref = jnp.where(row < 4, jnp.broadcast_to(v[1:2], (SUBLANES, w)),
                            jnp.broadcast_to(v[5:6], (SUBLANES, w)))
        else:
            ref = jnp.where((row & 1) == 1, pltpu.roll(v, 1, 0), v)
        parts.append(ref)
    return jnp.concatenate(parts, axis=0)


_NT = (((1,), (1,)), ((), ()))
_TN = (((0,), (0,)), ((), ()))


def _hgrn_unit(q_ref, f_ref, i_ref, g_ref, lb_ref, gain_ref, o_ref, st_scr, *, hb, tb):
    w = hb * HEAD
    lb = lb_ref[...]
    gain = gain_ref[...]
    ri = lax.broadcasted_iota(jnp.int32, (CHUNK, CHUNK), 0)
    ci = lax.broadcasted_iota(jnp.int32, (CHUNK, CHUNK), 1)
    roww = lax.broadcasted_iota(jnp.int32, (CHUNK, w), 0)
    levels = (32, 16, 8, 4, 2, 1)
    keeps = [(ri == ci).astype(F32)] + [
        None if 2 * s == CHUNK else ((ri & -(2 * s)) == (ci & -(2 * s))).astype(F32) for s in levels]
    second = [(roww & s) != 0 for s in levels]

    heads = [slice(h * HEAD, (h + 1) * HEAD) for h in range(hb)]

    prep = []
    for c in range(tb // CHUNK):
        rows = slice(c * CHUNK, (c + 1) * CHUNK)
        q, k, lf, fg = _hgrn_gates(q_ref[rows, :], f_ref[rows, :], lb)
        cum = _cumsum_rows(lf)
        last = jnp.broadcast_to(cum[CHUNK - 1:CHUNK], (CHUNK, w))
        q_in = (q * jnp.exp2(cum)).astype(BF16)
        k_out = (k * jnp.exp2(last - cum)).astype(BF16)
        qs, ks = [q.astype(BF16)], [k.astype(BF16)]
        for s, m in zip(levels, second):
            if s == 1:
                qs.append(jnp.where(m, q * fg, 0.0).astype(BF16))
                ks.append(jnp.where(m, 0.0, k).astype(BF16))
                continue
            ref = _level_reference(cum, s)
            e = jnp.exp2(jnp.where(m, cum - ref, ref - cum))
            qk = jnp.where(m, q, k) * e
            qs.append(jnp.where(m, qk, 0.0).astype(BF16))
            ks.append(jnp.where(m, 0.0, qk).astype(BF16))
        prep.append((q_in, k_out, i_ref[rows, :].astype(BF16), qs, ks, last))

    scores = []
    for _, _, _, qs, ks, _ in prep:
        per_head = []
        for sl in heads:
            a = None
            for keep, qt, kt in zip(keeps, qs, ks):
                p = lax.dot_general(qt[:, sl], kt[:, sl], _NT, preferred_element_type=F32)
                if keep is not None:
                    p = p * keep
                a = p if a is None else a + p
            per_head.append(a.astype(BF16))
        scores.append(per_head)

    state = [st_scr[h] for h in range(hb)]
    for c, (q_in, k_out, vb, _, _, last) in enumerate(prep):
        rows = slice(c * CHUNK, (c + 1) * CHUNK)
        outs = []
        for h, sl in enumerate(heads):
            o = lax.dot_general(q_in[:, sl], state[h].astype(BF16), _NT, preferred_element_type=F32)
            o = o + jnp.dot(scores[c][h], vb[:, sl], preferred_element_type=F32)
            state[h] = state[h] * jnp.exp2(last[0:1, sl]) + lax.dot_general(
                vb[:, sl], k_out[:, sl], _TN, preferred_element_type=F32)
            outs.append(o * lax.rsqrt(jnp.mean(o * o, axis=-1, keepdims=True) + EPS))
        graw = g_ref[rows, :]
        y = jnp.concatenate(outs, axis=1) * gain * (graw * _sigmoid(graw))
        o_ref[rows, :] = y.astype(o_ref.dtype)
    for h in range(hb):
        st_scr[h] = state[h]


def _hgrn_kernel(*refs, hb, tb, nblks):
    q_ref, f_ref, i_ref, g_ref, lb_ref, gain_ref, s0_ref = refs[:7]
    ns = len(nblks)
    side_in = refs[7:7 + ns]
    o_ref = refs[7 + ns]
    side_out = refs[8 + ns:8 + 2 * ns]
    st_scr = refs[-1]

    @pl.when(pl.program_id(2) == 0)
    def _():
        st_scr[...] = s0_ref[...]

    step = (pl.program_id(0) * pl.num_programs(1) + pl.program_id(1)) * pl.num_programs(2) + pl.program_id(2)
    _convert_side(step, side_in, side_out, nblks)
    _hgrn_unit(q_ref, f_ref, i_ref, g_ref, lb_ref, gain_ref, o_ref, st_scr, hb=hb, tb=tb)


def _hgrn(proj, lb, gain, s0, side=(), *, nb, hw, hb_want=4, tb_want=256):
    m = proj.shape[0]
    t = m // nb
    nh = hw // HEAD
    hb = _tile(nh, hb_want)
    w = hb * HEAD
    tb = _tile(t, tb_want)
    nt = t // tb
    ng = nh // hb

    def col(group):
        return lambda b, hg, tt: (b * nt + tt, group * ng + hg)

    side_specs, nblks = _side_specs(side, nb * ng * nt, lambda b, hg, tt: (b * ng + hg) * nt + tt)
    res = pl.pallas_call(
        functools.partial(_hgrn_kernel, hb=hb, tb=tb, nblks=tuple(nblks)),
        grid=(nb, ng, nt),
        in_specs=[
            pl.BlockSpec((tb, w), col(0)),
            pl.BlockSpec((tb, w), col(1)),
            pl.BlockSpec((tb, w), col(2)),
            pl.BlockSpec((tb, w), col(3)),
            pl.BlockSpec((1, w), lambda b, hg, tt: (0, hg)),
            pl.BlockSpec((1, w), lambda b, hg, tt: (0, hg)),
            pl.BlockSpec((hb, HEAD, HEAD), lambda b, hg, tt: (hg, 0, 0)),
        ] + side_specs,
        out_specs=[pl.BlockSpec((tb, w), lambda b, hg, tt: (b * nt + tt, hg))] + side_specs,
        out_shape=[jax.ShapeDtypeStruct((m, hw), BF16)] + _side_shapes(side),
        scratch_shapes=[pltpu.VMEM((hb, HEAD, HEAD), F32)],
        compiler_params=_params("arbitrary", "arbitrary", "arbitrary"),
        name="hgrn",
    )(proj, proj, proj, proj, lb, gain, s0, *side)
    return res[0], tuple(res[1:])


def _hgrn_meta_kernel(f_ref, i_ref, lb_ref, s_ref, *, nh):
    f_raw = f_ref[...]
    _, k, lf, _ = _hgrn_gates(f_raw, f_raw, lb_ref[...])
    cum = _cumsum_rows(lf)
    r, w = cum.shape
    last = jnp.broadcast_to(cum[r - 1:r], (r, w))
    k_out = (k * jnp.exp2(last - cum)).astype(BF16)
    vb = i_ref[...].astype(BF16)
    for h in range(nh):
        sl = slice(h * HEAD, (h + 1) * HEAD)
        s_ref[h] = lax.dot_general(vb[:, sl], k_out[:, sl], _TN, preferred_element_type=F32)


def _hgrn_meta(proj_meta, lb, *, hw):
    nmeta = proj_meta.shape[0]
    nh = hw // HEAD
    return pl.pallas_call(
        functools.partial(_hgrn_meta_kernel, nh=nh),
        grid=(1,),
        in_specs=[
            pl.BlockSpec((nmeta, hw), lambda i: (0, 1)),
            pl.BlockSpec((nmeta, hw), lambda i: (0, 2)),
            pl.BlockSpec((1, hw), lambda i: (0, 0)),
        ],
        out_specs=pl.BlockSpec((nh, HEAD, HEAD), lambda i: (0, 0, 0)),
        out_shape=jax.ShapeDtypeStruct((nh, HEAD, HEAD), F32),
        compiler_params=_params("arbitrary"),
        name="hgrn_meta",
    )(proj_meta, proj_meta, lb)


def _lru_unit(x_ref, g_ref, cw_ref, cb_ref, wa_ref, ba_ref, wx_ref, bx_ref, lam_ref, gain_ref, y_ref,
              xbuf, a_scr, b_scr, hcar, *, tb, nblk, blk):
    x = x_ref[...]
    c = x.shape[1]
    xbuf[SUBLANES:SUBLANES + tb, :] = x
    cw = cw_ref[...]
    xc = (cb_ref[...] + cw[3:4] * x
          + cw[2:3] * xbuf[SUBLANES - 1:SUBLANES - 1 + tb, :]
          + cw[1:2] * xbuf[SUBLANES - 2:SUBLANES - 2 + tb, :]
          + cw[0:1] * xbuf[SUBLANES - 3:SUBLANES - 3 + tb, :])
    xbuf[0:SUBLANES, :] = x[tb - SUBLANES:tb, :]

    xcb = xc.astype(BF16)
    nlam = -lam_ref[...]
    sp = jnp.maximum(nlam, 0.0) + jnp.log(1.0 + jnp.exp(-jnp.abs(nlam)))
    for hh in range(nblk):
        sl = slice(hh * blk, (hh + 1) * blk)
        r = _sigmoid(jnp.dot(xcb[:, sl], wa_ref[hh], preferred_element_type=F32) + ba_ref[:, sl])
        ig = _sigmoid(jnp.dot(xcb[:, sl], wx_ref[hh], preferred_element_type=F32) + bx_ref[:, sl])
        a = jnp.exp((-RG_C) * r * sp[:, sl])
        a_scr[:, sl] = a
        om = 1.0 - a * a
        root = jnp.where(om > 0.0, om * lax.rsqrt(om), 0.0)
        b_scr[:, sl] = root * (ig * xc[:, sl])

    row = lax.broadcasted_iota(jnp.int32, (SUBLANES, c), 0)

    def group(gi, hprev):
        r0 = pl.multiple_of(gi * SUBLANES, SUBLANES)
        av = a_scr[pl.ds(r0, SUBLANES), :]
        bv = b_scr[pl.ds(r0, SUBLANES), :]
        for dd in (1, 2, 4):
            keep = row >= dd
            bv = jnp.where(keep, av * pltpu.roll(bv, dd, 0) + bv, bv)
            av = jnp.where(keep, av * pltpu.roll(av, dd, 0), av)
        hv = av * hprev + bv
        b_scr[pl.ds(r0, SUBLANES), :] = hv
        return jnp.broadcast_to(hv[SUBLANES - 1:SUBLANES], (SUBLANES, c))

    hlast = lax.fori_loop(0, tb // SUBLANES, group, hcar[...])
    hcar[...] = hlast

    h = b_scr[...]
    gr = g_ref[...]
    gelu = 0.5 * gr * (1.0 + jnp.tanh(0.7978845608028654 * (gr + 0.044715 * (gr * gr * gr))))
    y_ref[...] = (_rms(h, gain_ref[...]) * gelu).astype(y_ref.dtype)
    return hlast


def _lru_kernel(*refs, tb, nblk, blk, nblks):
    (x_ref, g_ref, cw_ref, cb_ref, wa_ref, ba_ref, wx_ref, bx_ref, lam_ref, gain_ref, h0_ref, xt0_ref) = refs[:12]
    ns = len(nblks)
    side_in = refs[12:12 + ns]
    y_ref, hl_ref = refs[12 + ns:14 + ns]
    side_out = refs[14 + ns:14 + 2 * ns]
    xbuf, a_scr, b_scr, hcar = refs[-4:]
    t = pl.program_id(1)

    @pl.when(t == 0)
    def _():
        hcar[...] = h0_ref[...]
        xbuf[0:SUBLANES, :] = xt0_ref[...]

    _convert_side(pl.program_id(0) * pl.num_programs(1) + t, side_in, side_out, nblks)
    hlast = _lru_unit(x_ref, g_ref, cw_ref, cb_ref, wa_ref, ba_ref, wx_ref, bx_ref, lam_ref, gain_ref, y_ref,
                      xbuf, a_scr, b_scr, hcar, tb=tb, nblk=nblk, blk=blk)

    @pl.when(t == pl.num_programs(1) - 1)
    def _():
        hl_ref[0] = hlast


def _lru(proj, cw, cb, wa, ba, wx, bx, lam, gain, h0, xt0, side=(), *, nb, lw, xcol, tb_want=256):
    m = proj.shape[0]
    t = m // nb
    tb = _tile(t, tb_want)
    nt = t // tb
    nblk, blk, _ = wa.shape
    vec = pl.BlockSpec((1, lw), lambda b, tt: (0, 0))
    side_specs, nblks = _side_specs(side, nb * nt, lambda b, tt: b * nt + tt)
    res = pl.pallas_call(
        functools.partial(_lru_kernel, tb=tb, nblk=nblk, blk=blk, nblks=tuple(nblks)),
        grid=(nb, nt),
        in_specs=[
            pl.BlockSpec((tb, lw), lambda b, tt: (b * nt + tt, xcol)),
            pl.BlockSpec((tb, lw), lambda b, tt: (b * nt + tt, xcol + 1)),
            pl.BlockSpec((cw.shape[0], lw), lambda b, tt: (0, 0)),
            vec,
            pl.BlockSpec((nblk, blk, blk), lambda b, tt: (0, 0, 0)),
            vec,
            pl.BlockSpec((nblk, blk, blk), lambda b, tt: (0, 0, 0)),
            vec, vec, vec,
            pl.BlockSpec((SUBLANES, lw), lambda b, tt: (0, 0)),
            pl.BlockSpec((SUBLANES, lw), lambda b, tt: (0, 0)),
        ] + side_specs,
        out_specs=[
            pl.BlockSpec((tb, lw), lambda b, tt: (b * nt + tt, 0)),
            pl.BlockSpec((1, SUBLANES, lw), lambda b, tt: (b, 0, 0)),
        ] + side_specs,
        out_shape=[
            jax.ShapeDtypeStruct((m, lw), BF16),
            jax.ShapeDtypeStruct((nb, SUBLANES, lw), F32),
        ] + _side_shapes(side),
        scratch_shapes=[
            pltpu.VMEM((SUBLANES + tb, lw), F32),
            pltpu.VMEM((tb, lw), F32),
            pltpu.VMEM((tb, lw), F32),
            pltpu.VMEM((SUBLANES, lw), F32),
        ],
        compiler_params=_params("arbitrary", "arbitrary"),
        name="lru",
    )(proj, proj, cw, cb, wa, ba, wx, bx, lam, gain, h0, xt0, *side)
    return res[0], res[1], tuple(res[2:])


def kernel(x, meta_tokens, ffn1_pre_norm, ffn1_w_gate, ffn1_w_up, ffn1_w_down, ffn1_post_norm, mix_pre_norm, w_in, hgrn_lb_logits, hgrn_out_norm, lru_conv_w, lru_conv_b, lru_w_a, lru_b_a, lru_w_x, lru_b_x, lru_lambda, lru_out_norm, w_out, mix_post_norm, ffn2_pre_norm, ffn2_w_gate, ffn2_w_up, ffn2_w_down, ffn2_post_norm):
    nb, seq, d = x.shape
    hw = hgrn_out_norm.shape[1]
    lw = lru_out_norm.shape[1]
    assert ffn1_pre_norm.shape[0] == 1, "one layer"
    assert hw == lw and w_in.shape[2] == 4 * hw + 2 * lw and hw % HEAD == 0
    assert seq % CHUNK == 0 and meta_tokens.shape[0] % SUBLANES == 0

    def vec(p):
        return p[0].reshape(1, -1).astype(F32)

    lb = jnp.cumsum(jax.nn.softmax(hgrn_lb_logits.astype(F32), axis=0), axis=0)[0].reshape(1, hw)
    lru_p = (lru_conv_w[0].astype(F32), vec(lru_conv_b), lru_w_a[0].astype(BF16), vec(lru_b_a),
             lru_w_x[0].astype(BF16), vec(lru_b_x), vec(lru_lambda), vec(lru_out_norm))
    xcol = (4 * hw) // lw

    hm, ffn1_w = _ffn_convert(meta_tokens.astype(F32), vec(ffn1_pre_norm), ffn1_w_gate[0], ffn1_w_up[0],
                              ffn1_w_down[0], vec(ffn1_post_norm))
    pm, w_in_b, _ = _inproj(hm, vec(mix_pre_norm), w_in[0], bn_want=512)
    s_meta = _hgrn_meta(pm, lb, hw=hw)
    zeros8 = jnp.zeros((SUBLANES, lw), F32)
    _, h_meta, _ = _lru(pm, *lru_p, zeros8, zeros8, nb=1, lw=lw, xcol=xcol)
    xt_meta = pm[pm.shape[0] - SUBLANES:, 4 * hw:4 * hw + lw]

    h = x.reshape(nb * seq, d).astype(F32)
    h = _ffn(h, vec(ffn1_pre_norm), *ffn1_w, vec(ffn1_post_norm))
    proj, _, (wu2, wd2) = _inproj(h, vec(mix_pre_norm), w_in_b, (ffn2_w_up[0], ffn2_w_down[0]))
    y_h, (wg2,) = _hgrn(proj, lb, vec(hgrn_out_norm), s_meta, (ffn2_w_gate[0],), nb=nb, hw=hw)
    ffn2_w = (wg2, wu2, wd2)
    y_l, _, (w_out_b,) = _lru(proj, *lru_p, h_meta[0], xt_meta, (w_out[0],), nb=nb, lw=lw, xcol=xcol)
    h = _outproj(y_h, y_l, w_out_b, h, vec(mix_post_norm))
    h = _ffn(h, vec(ffn2_pre_norm), *ffn2_w, vec(ffn2_post_norm))
    return h.reshape(nb, seq, d).astype(x.dtype)
```

```python
import functools

import jax
import jax.numpy as jnp
from jax import lax
from jax.experimental import pallas as pl
from jax.experimental.pallas import tpu as pltpu

F32 = jnp.float32
BF16 = jnp.bfloat16

EPS = 1e-6
RG_C = 8.0
LOG2E = 1.4426950408889634
HEAD = 128
CHUNK = 64
SUBLANES = 8
BF16_ROWS = 16
V7X_VMEM_LIMIT_BYTES = 60 * 1024 * 1024


def _tile(n, want):
    t = min(n, want)
    while n % t:
        t -= 1
    return t


def _sigmoid(x):
    return 0.5 * jnp.tanh(0.5 * x) + 0.5


def _rms(x, gain):
    return x * lax.rsqrt(jnp.mean(x * x, axis=-1, keepdims=True) + EPS) * gain


ROW_CHUNK = 256


def _prenorm_rows(h_ref, gain_ref, u_ref):
    step = _tile(h_ref.shape[0], ROW_CHUNK)
    for r0 in range(0, h_ref.shape[0], step):
        u_ref[r0:r0 + step, :] = _rms(h_ref[r0:r0 + step, :], gain_ref[...]).astype(u_ref.dtype)


def _postnorm_rows(h_ref, gain_ref, o_ref, scale):
    step = _tile(o_ref.shape[0], ROW_CHUNK)
    for r0 in range(0, o_ref.shape[0], step):
        o_ref[r0:r0 + step, :] = h_ref[r0:r0 + step, :] + scale * _rms(o_ref[r0:r0 + step, :], gain_ref[...])


def _params(*sem):
    return pltpu.CompilerParams(dimension_semantics=sem, vmem_limit_bytes=V7X_VMEM_LIMIT_BYTES)


def _side_specs(side, n_steps, linear):
    specs, nblks = [], []
    for arr in side:
        rows, cols = arr.shape
        assert rows % BF16_ROWS == 0
        nblk = max(k for k in range(1, n_steps + 1) if (rows // BF16_ROWS) % k == 0)
        specs.append(pl.BlockSpec((rows // nblk, cols),
                                  lambda *ids, _n=nblk: (jnp.minimum(linear(*ids), _n - 1), 0)))
        nblks.append(nblk)
    return specs, nblks


def _convert_side(step, side_in, side_out, nblks):
    for src, dst, nblk in zip(side_in, side_out, nblks):
        @pl.when(step < nblk)
        def _():
            dst[...] = src[...].astype(dst.dtype)


def _side_shapes(side):
    return [jax.ShapeDtypeStruct(arr.shape, BF16) for arr in side]


def _swiglu_accumulate(u, wg, wu, wd_ref, rows, o_ref, col_chunk, assign):
    g = jnp.dot(u, wg, preferred_element_type=F32)
    v = jnp.dot(u, wu, preferred_element_type=F32)
    a = (g * jax.nn.sigmoid(g) * v).astype(BF16)
    for c0 in range(0, o_ref.shape[1], col_chunk):
        part = jnp.dot(a, wd_ref[rows, c0:c0 + col_chunk], preferred_element_type=F32)
        if assign:
            o_ref[:, c0:c0 + col_chunk] = part
        else:
            o_ref[:, c0:c0 + col_chunk] += part


def _ffn_kernel(h_ref, gpre_ref, wg_ref, wu_ref, wd_ref, gpost_ref, o_ref, u_scr, *, first, col_chunk):
    j = pl.program_id(1)
    win = wg_ref.shape[1]
    nf = first * (win // 2)

    @pl.when(j == 0)
    def _():
        _prenorm_rows(h_ref, gpre_ref, u_scr)
        _swiglu_accumulate(u_scr[...], wg_ref[:, 0:nf], wu_ref[:, 0:nf], wd_ref, slice(0, nf), o_ref,
                           col_chunk, True)

    @pl.when(j > 0)
    def _():
        _swiglu_accumulate(u_scr[...], wg_ref[...], wu_ref[...], wd_ref, slice(0, win), o_ref, col_chunk, False)

    @pl.when(j == pl.num_programs(1) - 1)
    def _():
        _postnorm_rows(h_ref, gpost_ref, o_ref, 0.5)


def _ffn(h, gpre, wg, wu, wd, gpost, *, bm_want=512, bf_want=256):
    m, d = h.shape
    f = wg.shape[1]
    bm = _tile(m, bm_want)
    bf = _tile(f, bf_want)
    nt = f // bf
    assert nt >= 2 and wg.dtype == BF16
    first = 1 if nt % 2 else 2
    nj = 1 + (nt - first) // 2

    def off(j):
        return jnp.maximum(first + 2 * (j - 1), 0) * bf

    return pl.pallas_call(
        functools.partial(_ffn_kernel, first=first, col_chunk=_tile(d, 512)),
        grid=(m // bm, nj),
        in_specs=[
            pl.BlockSpec((bm, d), lambda i, j: (i, 0), pipeline_mode=pl.Buffered(1)),
            pl.BlockSpec((1, d), lambda i, j: (0, 0)),
            pl.BlockSpec((pl.Element(d), pl.Element(2 * bf)), lambda i, j: (0, off(j))),
            pl.BlockSpec((pl.Element(d), pl.Element(2 * bf)), lambda i, j: (0, off(j))),
            pl.BlockSpec((pl.Element(2 * bf), pl.Element(d)), lambda i, j: (off(j), 0)),
            pl.BlockSpec((1, d), lambda i, j: (0, 0)),
        ],
        out_specs=pl.BlockSpec((bm, d), lambda i, j: (i, 0), pipeline_mode=pl.Buffered(1)),
        out_shape=jax.ShapeDtypeStruct((m, d), F32),
        scratch_shapes=[pltpu.VMEM((bm, d), BF16)],
        compiler_params=_params("parallel", "arbitrary"),
        name="ffn",
    )(h, gpre, wg, wu, wd, gpost)


def _ffn_convert_kernel(h_ref, gpre_ref, wg_ref, wu_ref, wd_ref, gpost_ref, o_ref, wgb_ref, wub_ref, wdb_ref, u_scr,
                        *, col_chunk):
    j = pl.program_id(0)

    @pl.when(j == 0)
    def _():
        _prenorm_rows(h_ref, gpre_ref, u_scr)
        o_ref[...] = jnp.zeros_like(o_ref)

    wgb_ref[...] = wg_ref[...].astype(BF16)
    wub_ref[...] = wu_ref[...].astype(BF16)
    wdb_ref[...] = wd_ref[...].astype(BF16)
    _swiglu_accumulate(u_scr[...], wgb_ref[...], wub_ref[...], wdb_ref, slice(None), o_ref, col_chunk, False)

    @pl.when(j == pl.num_programs(0) - 1)
    def _():
        _postnorm_rows(h_ref, gpost_ref, o_ref, 0.5)


def _ffn_convert(h, gpre, wg, wu, wd, gpost, *, bf_want=256):
    m, d = h.shape
    f = wg.shape[1]
    bf = _tile(f, bf_want)
    cols = pl.BlockSpec((d, bf), lambda j: (0, j))
    rows = pl.BlockSpec((bf, d), lambda j: (j, 0))
    full = pl.BlockSpec((m, d), lambda j: (0, 0))
    vec = pl.BlockSpec((1, d), lambda j: (0, 0))
    res = pl.pallas_call(
        functools.partial(_ffn_convert_kernel, col_chunk=_tile(d, 512)),
        grid=(f // bf,),
        in_specs=[full, vec, cols, cols, rows, vec],
        out_specs=[full, cols, cols, rows],
        out_shape=[jax.ShapeDtypeStruct((m, d), F32)] + [jax.ShapeDtypeStruct(w.shape, BF16) for w in (wg, wu, wd)],
        scratch_shapes=[pltpu.VMEM((m, d), BF16)],
        compiler_params=_params("arbitrary"),
        name="ffn_convert",
    )(h, gpre, wg, wu, wd, gpost)
    return res[0], tuple(res[1:])


def _inproj_kernel(*refs, emit, nblks):
    h_ref, g_ref, w_ref = refs[:3]
    ns = len(nblks)
    side_in = refs[3:3 + ns]
    o_ref = refs[3 + ns]
    side_out = refs[len(refs) - 1 - ns:len(refs) - 1]
    u_scr = refs[-1]

    @pl.when(pl.program_id(1) == 0)
    def _():
        _prenorm_rows(h_ref, g_ref, u_scr)

    _convert_side(pl.program_id(0) * pl.num_programs(1) + pl.program_id(1), side_in, side_out, nblks)
    if emit:
        wb_ref = refs[4 + ns]
        wb_ref[...] = w_ref[...].astype(BF16)
        w_ref = wb_ref
    o_ref[...] = jnp.dot(u_scr[...], w_ref[...], preferred_element_type=F32)


def _inproj(h, gain, w, side=(), *, bm_want=1024, bn_want=512):
    m, d = h.shape
    n = w.shape[1]
    bm = _tile(m, bm_want)
    bn = _tile(n, bn_want)
    nj = n // bn
    emit = w.dtype != BF16
    out_specs = [pl.BlockSpec((bm, bn), lambda i, j: (i, j))]
    out_shape = [jax.ShapeDtypeStruct((m, n), F32)]
    if emit:
        assert m == bm, "weights are converted on a single pass over the column tiles"
        out_specs.append(pl.BlockSpec((d, bn), lambda i, j: (0, j)))
        out_shape.append(jax.ShapeDtypeStruct(w.shape, BF16))
    side_specs, nblks = _side_specs(side, (m // bm) * nj, lambda i, j: i * nj + j)
    res = pl.pallas_call(
        functools.partial(_inproj_kernel, emit=emit, nblks=tuple(nblks)),
        grid=(m // bm, nj),
        in_specs=[
            pl.BlockSpec((bm, d), lambda i, j: (i, 0), pipeline_mode=pl.Buffered(1)),
            pl.BlockSpec((1, d), lambda i, j: (0, 0)),
            pl.BlockSpec((d, bn), lambda i, j: (0, j)),
        ] + side_specs,
        out_specs=out_specs + side_specs,
        out_shape=out_shape + _side_shapes(side),
        scratch_shapes=[pltpu.VMEM((bm, d), BF16)],
        compiler_params=_params("arbitrary", "arbitrary"),
        name="inproj_convert" if emit else "inproj",
    )(h, gain, w, *side)
    return res[0], (res[1] if emit else w), tuple(res[len(res) - len(side):]) if side else ()


def _outproj_kernel(yh_ref, yl_ref, w_ref, h_ref, g_ref, o_ref, *, col_chunk):
    kh = yh_ref.shape[1]
    yh = yh_ref[...]
    yl = yl_ref[...]
    for c0 in range(0, o_ref.shape[1], col_chunk):
        o_ref[:, c0:c0 + col_chunk] = (
            jnp.dot(yh, w_ref[0:kh, c0:c0 + col_chunk], preferred_element_type=F32)
            + jnp.dot(yl, w_ref[kh:2 * kh, c0:c0 + col_chunk], preferred_element_type=F32))
    _postnorm_rows(h_ref, g_ref, o_ref, 1.0)


def _outproj(yh, yl, w, h, gain, *, bm_want=128):
    m, kh = yh.shape
    d = w.shape[1]
    bm = _tile(m, bm_want)
    return pl.pallas_call(
        functools.partial(_outproj_kernel, col_chunk=_tile(d, 1024)),
        grid=(m // bm,),
        in_specs=[
            pl.BlockSpec((bm, kh), lambda i: (i, 0)),
            pl.BlockSpec((bm, kh), lambda i: (i, 0)),
            pl.BlockSpec((2 * kh, d), lambda i: (0, 0), pipeline_mode=pl.Buffered(1)),
            pl.BlockSpec((bm, d), lambda i: (i, 0)),
            pl.BlockSpec((1, d), lambda i: (0, 0)),
        ],
        out_specs=pl.BlockSpec((bm, d), lambda i: (i, 0)),
        out_shape=jax.ShapeDtypeStruct((m, d), F32),
        compiler_params=_params("parallel"),
        name="outproj",
    )(yh, yl, w, h, gain)


def _cumsum_rows(x):
    r, w = x.shape
    row = lax.broadcasted_iota(jnp.int32, (SUBLANES, w), 0)
    out = []
    carry = None
    for g in range(r // SUBLANES):
        v = x[g * SUBLANES:(g + 1) * SUBLANES]
        for dd in (1, 2, 4):
            v = v + jnp.where(row >= dd, pltpu.roll(v, dd, 0), 0.0)
        if carry is not None:
            v = v + carry
        carry = jnp.broadcast_to(v[SUBLANES - 1:SUBLANES], (SUBLANES, w))
        out.append(v)
    return jnp.concatenate(out, axis=0)


def _hgrn_gates(q_raw, f_raw, lb):
    q = q_raw * _sigmoid(q_raw)
    sf = _sigmoid(f_raw)
    fg = lb + (1.0 - lb) * sf
    k = (1.0 - lb) * (1.0 - sf)
    return q, k, jnp.log(fg) * LOG2E, fg


def _level_reference(cum, s):
    r, w = cum.shape
    if s >= SUBLANES:
        parts = [jnp.broadcast_to(cum[p + s - 1:p + s], (2 * s, w)) for p in range(0, r, 2 * s)]
        return jnp.concatenate(parts, axis=0)
    row = lax.broadcasted_iota(jnp.int32, (SUBLANES, w), 0)
    parts = []
    for g in range(r // SUBLANES):
        v = cum[g * SUBLANES:(g + 1) * SUBLANES]
        if s == 4:
            ref = jnp.broadcast_to(v[3:4], (SUBLANES, w))
        elif s == 2:
            ref = jnp.where(row < 4, jnp.broadcast_to(v[1:2], (SUBLANES, w)),
                            jnp.broadcast_to(v[5:6], (SUBLANES, w)))
        else:
            ref = jnp.where((row & 1) == 1, pltpu.roll(v, 1, 0), v)
        parts.append(ref)
    return jnp.concatenate(parts, axis=0)


_NT = (((1,), (1,)), ((), ()))
_TN = (((0,), (0,)), ((), ()))


def _hgrn_unit(q_ref, f_ref, i_ref, g_ref, lb_ref, gain_ref, o_ref, st_scr, *, hb, tb):
    w = hb * HEAD
    lb = lb_ref[...]
    gain = gain_ref[...]
    ri = lax.broadcasted_iota(jnp.int32, (CHUNK, CHUNK), 0)
    ci = lax.broadcasted_iota(jnp.int32, (CHUNK, CHUNK), 1)
    roww = lax.broadcasted_iota(jnp.int32, (CHUNK, w), 0)
    levels = (32, 16, 8, 4, 2, 1)
    keeps = [(ri == ci).astype(F32)] + [
        None if 2 * s == CHUNK else ((ri & -(2 * s)) == (ci & -(2 * s))).astype(F32) for s in levels]
    second = [(roww & s) != 0 for s in levels]

    heads = [slice(h * HEAD, (h + 1) * HEAD) for h in range(hb)]

    prep = []
    for c in range(tb // CHUNK):
        rows = slice(c * CHUNK, (c + 1) * CHUNK)
        q, k, lf, fg = _hgrn_gates(q_ref[rows, :], f_ref[rows, :], lb)
        cum = _cumsum_rows(lf)
        last = jnp.broadcast_to(cum[CHUNK - 1:CHUNK], (CHUNK, w))
        q_in = (q * jnp.exp2(cum)).astype(BF16)
        k_out = (k * jnp.exp2(last - cum)).astype(BF16)
        qs, ks = [q.astype(BF16)], [k.astype(BF16)]
        for s, m in zip(levels, second):
            if s == 1:
                qs.append(jnp.where(m, q * fg, 0.0).astype(BF16))
                ks.append(jnp.where(m, 0.0, k).astype(BF16))
                continue
            ref = _level_reference(cum, s)
            e = jnp.exp2(jnp.where(m, cum - ref, ref - cum))
            qk = jnp.where(m, q, k) * e
            qs.append(jnp.where(m, qk, 0.0).astype(BF16))
            ks.append(jnp.where(m, 0.0, qk).astype(BF16))
        prep.append((q_in, k_out, i_ref[rows, :].astype(BF16), qs, ks, last))

    scores = []
    for _, _, _, qs, ks, _ in prep:
        per_head = []
        for sl in heads:
            a = None
            for keep, qt, kt in zip(keeps, qs, ks):
                p = lax.dot_general(qt[:, sl], kt[:, sl], _NT, preferred_element_type=F32)
                if keep is not None:
                    p = p * keep
                a = p if a is None else a + p
            per_head.append(a.astype(BF16))
        scores.append(per_head)

    state = [st_scr[h] for h in range(hb)]
    for c, (q_in, k_out, vb, _, _, last) in enumerate(prep):
        rows = slice(c * CHUNK, (c + 1) * CHUNK)
        outs = []
        for h, sl in enumerate(heads):
            o = lax.dot_general(q_in[:, sl], state[h].astype(BF16), _NT, preferred_element_type=F32)
            o = o + jnp.dot(scores[c][h], vb[:, sl], preferred_element_type=F32)
            state[h] = state[h] * jnp.exp2(last[0:1, sl]) + lax.dot_general(
                vb[:, sl], k_out[:, sl], _TN, preferred_element_type=F32)
            outs.append(o * lax.rsqrt(jnp.mean(o * o, axis=-1, keepdims=True) + EPS))
        graw = g_ref[rows, :]
        y = jnp.concatenate(outs, axis=1) * gain * (graw * _sigmoid(graw))
        o_ref[rows, :] = y.astype(o_ref.dtype)
    for h in range(hb):
        st_scr[h] = state[h]


def _hgrn_kernel(q_ref, f_ref, i_ref, g_ref, lb_ref, gain_ref, s0_ref, o_ref, st_scr, *, hb, tb):
    @pl.when(pl.program_id(2) == 0)
    def _():
        st_scr[...] = s0_ref[...]

    _hgrn_unit(q_ref, f_ref, i_ref, g_ref, lb_ref, gain_ref, o_ref, st_scr, hb=hb, tb=tb)


def _hgrn(proj, lb, gain, s0, *, nb, hw, hb_want=4, tb_want=256):
    m = proj.shape[0]
    t = m // nb
    nh = hw // HEAD
    hb = _tile(nh, hb_want)
    w = hb * HEAD
    tb = _tile(t, tb_want)
    nt = t // tb
    ng = nh // hb

    def col(group):
        return lambda b, hg, tt: (b * nt + tt, group * ng + hg)

    return pl.pallas_call(
        functools.partial(_hgrn_kernel, hb=hb, tb=tb),
        grid=(nb, ng, nt),
        in_specs=[
            pl.BlockSpec((tb, w), col(0)),
            pl.BlockSpec((tb, w), col(1)),
            pl.BlockSpec((tb, w), col(2)),
            pl.BlockSpec((tb, w), col(3)),
            pl.BlockSpec((1, w), lambda b, hg, tt: (0, hg)),
            pl.BlockSpec((1, w), lambda b, hg, tt: (0, hg)),
            pl.BlockSpec((hb, HEAD, HEAD), lambda b, hg, tt: (hg, 0, 0)),
        ],
        out_specs=pl.BlockSpec((tb, w), lambda b, hg, tt: (b * nt + tt, hg)),
        out_shape=jax.ShapeDtypeStruct((m, hw), BF16),
        scratch_shapes=[pltpu.VMEM((hb, HEAD, HEAD), F32)],
        compiler_params=_params("parallel", "parallel", "arbitrary"),
        name="hgrn",
    )(proj, proj, proj, proj, lb, gain, s0)


def _hgrn_meta_kernel(f_ref, i_ref, lb_ref, s_ref, *, nh):
    f_raw = f_ref[...]
    _, k, lf, _ = _hgrn_gates(f_raw, f_raw, lb_ref[...])
    cum = _cumsum_rows(lf)
    r, w = cum.shape
    last = jnp.broadcast_to(cum[r - 1:r], (r, w))
    k_out = (k * jnp.exp2(last - cum)).astype(BF16)
    vb = i_ref[...].astype(BF16)
    for h in range(nh):
        sl = slice(h * HEAD, (h + 1) * HEAD)
        s_ref[h] = lax.dot_general(vb[:, sl], k_out[:, sl], _TN, preferred_element_type=F32)


def _hgrn_meta(proj_meta, lb, *, hw):
    nmeta = proj_meta.shape[0]
    nh = hw // HEAD
    return pl.pallas_call(
        functools.partial(_hgrn_meta_kernel, nh=nh),
        grid=(1,),
        in_specs=[
            pl.BlockSpec((nmeta, hw), lambda i: (0, 1)),
            pl.BlockSpec((nmeta, hw), lambda i: (0, 2)),
            pl.BlockSpec((1, hw), lambda i: (0, 0)),
        ],
        out_specs=pl.BlockSpec((nh, HEAD, HEAD), lambda i: (0, 0, 0)),
        out_shape=jax.ShapeDtypeStruct((nh, HEAD, HEAD), F32),
        compiler_params=_params("arbitrary"),
        name="hgrn_meta",
    )(proj_meta, proj_meta, lb)


def _lru_unit(x_ref, g_ref, cw_ref, cb_ref, wa_ref, ba_ref, wx_ref, bx_ref, lam_ref, gain_ref, y_ref,
              xbuf, a_scr, b_scr, hcar, *, tb, nblk, blk):
    x = x_ref[...]
    c = x.shape[1]
    xbuf[SUBLANES:SUBLANES + tb, :] = x
    cw = cw_ref[...]
    xc = (cb_ref[...] + cw[3:4] * x
          + cw[2:3] * xbuf[SUBLANES - 1:SUBLANES - 1 + tb, :]
          + cw[1:2] * xbuf[SUBLANES - 2:SUBLANES - 2 + tb, :]
          + cw[0:1] * xbuf[SUBLANES - 3:SUBLANES - 3 + tb, :])
    xbuf[0:SUBLANES, :] = x[tb - SUBLANES:tb, :]

    xcb = xc.astype(BF16)
    nlam = -lam_ref[...]
    sp = jnp.maximum(nlam, 0.0) + jnp.log(1.0 + jnp.exp(-jnp.abs(nlam)))
    for hh in range(nblk):
        sl = slice(hh * blk, (hh + 1) * blk)
        r = _sigmoid(jnp.dot(xcb[:, sl], wa_ref[hh], preferred_element_type=F32) + ba_ref[:, sl])
        ig = _sigmoid(jnp.dot(xcb[:, sl], wx_ref[hh], preferred_element_type=F32) + bx_ref[:, sl])
        a = jnp.exp((-RG_C) * r * sp[:, sl])
        a_scr[:, sl] = a
        om = 1.0 - a * a
        root = jnp.where(om > 0.0, om * lax.rsqrt(om), 0.0)
        b_scr[:, sl] = root * (ig * xc[:, sl])

    row = lax.broadcasted_iota(jnp.int32, (SUBLANES, c), 0)

    def group(gi, hprev):
        r0 = pl.multiple_of(gi * SUBLANES, SUBLANES)
        av = a_scr[pl.ds(r0, SUBLANES), :]
        bv = b_scr[pl.ds(r0, SUBLANES), :]
        for dd in (1, 2, 4):
            keep = row >= dd
            bv = jnp.where(keep, av * pltpu.roll(bv, dd, 0) + bv, bv)
            av = jnp.where(keep, av * pltpu.roll(av, dd, 0), av)
        hv = av * hprev + bv
        b_scr[pl.ds(r0, SUBLANES), :] = hv
        return jnp.broadcast_to(hv[SUBLANES - 1:SUBLANES], (SUBLANES, c))

    hlast = lax.fori_loop(0, tb // SUBLANES, group, hcar[...])
    hcar[...] = hlast

    h = b_scr[...]
    gr = g_ref[...]
    gelu = 0.5 * gr * (1.0 + jnp.tanh(0.7978845608028654 * (gr + 0.044715 * (gr * gr * gr))))
    y_ref[...] = (_rms(h, gain_ref[...]) * gelu).astype(y_ref.dtype)
    return hlast


def _lru_kernel(*refs, tb, nblk, blk, nblks):
    (x_ref, g_ref, cw_ref, cb_ref, wa_ref, ba_ref, wx_ref, bx_ref, lam_ref, gain_ref, h0_ref, xt0_ref) = refs[:12]
    ns = len(nblks)
    side_in = refs[12:12 + ns]
    y_ref, hl_ref = refs[12 + ns:14 + ns]
    side_out = refs[14 + ns:14 + 2 * ns]
    xbuf, a_scr, b_scr, hcar = refs[-4:]
    t = pl.program_id(1)

    @pl.when(t == 0)
    def _():
        hcar[...] = h0_ref[...]
        xbuf[0:SUBLANES, :] = xt0_ref[...]

    _convert_side(pl.program_id(0) * pl.num_programs(1) + t, side_in, side_out, nblks)
    hlast = _lru_unit(x_ref, g_ref, cw_ref, cb_ref, wa_ref, ba_ref, wx_ref, bx_ref, lam_ref, gain_ref, y_ref,
                      xbuf, a_scr, b_scr, hcar, tb=tb, nblk=nblk, blk=blk)

    @pl.when(t == pl.num_programs(1) - 1)
    def _():
        hl_ref[0] = hlast


def _lru(proj, cw, cb, wa, ba, wx, bx, lam, gain, h0, xt0, side=(), *, nb, lw, xcol, tb_want=256):
    m = proj.shape[0]
    t = m // nb
    tb = _tile(t, tb_want)
    nt = t // tb
    nblk, blk, _ = wa.shape
    vec = pl.BlockSpec((1, lw), lambda b, tt: (0, 0))
    side_specs, nblks = _side_specs(side, nb * nt, lambda b, tt: b * nt + tt)
    res = pl.pallas_call(
        functools.partial(_lru_kernel, tb=tb, nblk=nblk, blk=blk, nblks=tuple(nblks)),
        grid=(nb, nt),
        in_specs=[
            pl.BlockSpec((tb, lw), lambda b, tt: (b * nt + tt, xcol)),
            pl.BlockSpec((tb, lw), lambda b, tt: (b * nt + tt, xcol + 1)),
            pl.BlockSpec((cw.shape[0], lw), lambda b, tt: (0, 0)),
            vec,
            pl.BlockSpec((nblk, blk, blk), lambda b, tt: (0, 0, 0)),
            vec,
            pl.BlockSpec((nblk, blk, blk), lambda b, tt: (0, 0, 0)),
            vec, vec, vec,
            pl.BlockSpec((SUBLANES, lw), lambda b, tt: (0, 0)),
            pl.BlockSpec((SUBLANES, lw), lambda b, tt: (0, 0)),
        ] + side_specs,
        out_specs=[
            pl.BlockSpec((tb, lw), lambda b, tt: (b * nt + tt, 0)),
            pl.BlockSpec((1, SUBLANES, lw), lambda b, tt: (b, 0, 0)),
        ] + side_specs,
        out_shape=[
            jax.ShapeDtypeStruct((m, lw), BF16),
            jax.ShapeDtypeStruct((nb, SUBLANES, lw), F32),
        ] + _side_shapes(side),
        scratch_shapes=[
            pltpu.VMEM((SUBLANES + tb, lw), F32),
            pltpu.VMEM((tb, lw), F32),
            pltpu.VMEM((tb, lw), F32),
            pltpu.VMEM((SUBLANES, lw), F32),
        ],
        compiler_params=_params("arbitrary", "arbitrary"),
        name="lru",
    )(proj, proj, cw, cb, wa, ba, wx, bx, lam, gain, h0, xt0, *side)
    return res[0], res[1], tuple(res[2:])


def kernel(x, meta_tokens, ffn1_pre_norm, ffn1_w_gate, ffn1_w_up, ffn1_w_down, ffn1_post_norm, mix_pre_norm, w_in, hgrn_lb_logits, hgrn_out_norm, lru_conv_w, lru_conv_b, lru_w_a, lru_b_a, lru_w_x, lru_b_x, lru_lambda, lru_out_norm, w_out, mix_post_norm, ffn2_pre_norm, ffn2_w_gate, ffn2_w_up, ffn2_w_down, ffn2_post_norm):
    nb, seq, d = x.shape
    hw = hgrn_out_norm.shape[1]
    lw = lru_out_norm.shape[1]
    assert ffn1_pre_norm.shape[0] == 1, "one layer"
    assert hw == lw and w_in.shape[2] == 4 * hw + 2 * lw and hw % HEAD == 0
    assert seq % CHUNK == 0 and meta_tokens.shape[0] % SUBLANES == 0

    def vec(p):
        return p[0].reshape(1, -1).astype(F32)

    lb = jnp.cumsum(jax.nn.softmax(hgrn_lb_logits.astype(F32), axis=0), axis=0)[0].reshape(1, hw)
    lru_p = (lru_conv_w[0].astype(F32), vec(lru_conv_b), lru_w_a[0].astype(BF16), vec(lru_b_a),
             lru_w_x[0].astype(BF16), vec(lru_b_x), vec(lru_lambda), vec(lru_out_norm))
    xcol = (4 * hw) // lw

    hm, ffn1_w = _ffn_convert(meta_tokens.astype(F32), vec(ffn1_pre_norm), ffn1_w_gate[0], ffn1_w_up[0],
                              ffn1_w_down[0], vec(ffn1_post_norm))
    pm, w_in_b, _ = _inproj(hm, vec(mix_pre_norm), w_in[0], bn_want=512)
    s_meta = _hgrn_meta(pm, lb, hw=hw)
    zeros8 = jnp.zeros((SUBLANES, lw), F32)
    _, h_meta, _ = _lru(pm, *lru_p, zeros8, zeros8, nb=1, lw=lw, xcol=xcol)
    xt_meta = pm[pm.shape[0] - SUBLANES:, 4 * hw:4 * hw + lw]

    h = x.reshape(nb * seq, d).astype(F32)
    h = _ffn(h, vec(ffn1_pre_norm), *ffn1_w, vec(ffn1_post_norm))
    proj, _, ffn2_w = _inproj(h, vec(mix_pre_norm), w_in_b, (ffn2_w_gate[0], ffn2_w_up[0], ffn2_w_down[0]))
    y_h = _hgrn(proj, lb, vec(hgrn_out_norm), s_meta, nb=nb, hw=hw)
    y_l, _, (w_out_b,) = _lru(proj, *lru_p, h_meta[0], xt_meta, (w_out[0],), nb=nb, lw=lw, xcol=xcol)
    h = _outproj(y_h, y_l, w_out_b, h, vec(mix_post_norm))
    h = _ffn(h, vec(ffn2_pre_norm), *ffn2_w, vec(ffn2_post_norm))
    return h.reshape(nb, seq, d).astype(x.dtype)
```

```python
import functools

import jax
import jax.numpy as jnp
from jax import lax
from jax.experimental import pallas as pl
from jax.experimental.pallas import tpu as pltpu

F32 = jnp.float32
BF16 = jnp.bfloat16

EPS = 1e-6
RG_C = 8.0
LOG2E = 1.4426950408889634
GELU_K = 0.7978845608028654
HEAD = 128
CHUNK = 64
SUBLANES = 8
BF16_ROWS = 16
V7X_VMEM_LIMIT_BYTES = 60 * 1024 * 1024


def _tile(n, want):
    t = min(n, want)
    while n % t:
        t -= 1
    return t


def _sigmoid(x):
    return 0.5 * jnp.tanh(0.5 * x) + 0.5


def _rms(x, gain):
    return x * lax.rsqrt(jnp.mean(x * x, axis=-1, keepdims=True) + EPS) * gain


ROW_CHUNK = 256


def _prenorm_rows(h_ref, gain_ref, u_ref):
    step = _tile(h_ref.shape[0], ROW_CHUNK)
    for r0 in range(0, h_ref.shape[0], step):
        u_ref[r0:r0 + step, :] = _rms(h_ref[r0:r0 + step, :], gain_ref[...]).astype(u_ref.dtype)


def _postnorm_rows(h_ref, gain_ref, o_ref, scale):
    step = _tile(o_ref.shape[0], ROW_CHUNK)
    for r0 in range(0, o_ref.shape[0], step):
        o_ref[r0:r0 + step, :] = h_ref[r0:r0 + step, :] + scale * _rms(o_ref[r0:r0 + step, :], gain_ref[...])


def _params(*sem):
    return pltpu.CompilerParams(dimension_semantics=sem, vmem_limit_bytes=V7X_VMEM_LIMIT_BYTES)


def _side_specs(side, n_steps, linear):
    specs, nblks = [], []
    for arr in side:
        rows, cols = arr.shape
        assert rows % BF16_ROWS == 0
        nblk = max(k for k in range(1, n_steps + 1) if (rows // BF16_ROWS) % k == 0)
        specs.append(pl.BlockSpec((rows // nblk, cols),
                                  lambda *ids, _n=nblk: (jnp.minimum(linear(*ids), _n - 1), 0)))
        nblks.append(nblk)
    return specs, nblks


def _convert_side(step, side_in, side_out, nblks):
    for src, dst, nblk in zip(side_in, side_out, nblks):
        @pl.when(step < nblk)
        def _():
            dst[...] = src[...].astype(dst.dtype)


def _side_shapes(side):
    return [jax.ShapeDtypeStruct(arr.shape, BF16) for arr in side]


def _swiglu_accumulate(u, wg, wu, wd_ref, rows, o_ref, col_chunk, assign):
    g = jnp.dot(u, wg, preferred_element_type=F32)
    v = jnp.dot(u, wu, preferred_element_type=F32)
    a = (g * jax.nn.sigmoid(g) * v).astype(BF16)
    for c0 in range(0, o_ref.shape[1], col_chunk):
        part = jnp.dot(a, wd_ref[rows, c0:c0 + col_chunk], preferred_element_type=F32)
        if assign:
            o_ref[:, c0:c0 + col_chunk] = part
        else:
            o_ref[:, c0:c0 + col_chunk] += part


def _ffn_kernel(h_ref, gpre_ref, wg_ref, wu_ref, wd_ref, gpost_ref, o_ref, u_scr, *, first, col_chunk):
    j = pl.program_id(1)
    win = wg_ref.shape[1]
    nf = first * (win // 2)

    @pl.when(j == 0)
    def _():
        _prenorm_rows(h_ref, gpre_ref, u_scr)
        _swiglu_accumulate(u_scr[...], wg_ref[:, 0:nf], wu_ref[:, 0:nf], wd_ref, slice(0, nf), o_ref,
                           col_chunk, True)

    @pl.when(j > 0)
    def _():
        _swiglu_accumulate(u_scr[...], wg_ref[...], wu_ref[...], wd_ref, slice(0, win), o_ref, col_chunk, False)

    @pl.when(j == pl.num_programs(1) - 1)
    def _():
        _postnorm_rows(h_ref, gpost_ref, o_ref, 0.5)


def _ffn(h, gpre, wg, wu, wd, gpost, *, bm_want=512, bf_want=256):
    m, d = h.shape
    f = wg.shape[1]
    bm = _tile(m, bm_want)
    bf = _tile(f, bf_want)
    nt = f // bf
    assert nt >= 2 and wg.dtype == BF16
    first = 1 if nt % 2 else 2
    nj = 1 + (nt - first) // 2

    def off(j):
        return jnp.maximum(first + 2 * (j - 1), 0) * bf

    return pl.pallas_call(
        functools.partial(_ffn_kernel, first=first, col_chunk=_tile(d, 512)),
        grid=(m // bm, nj),
        in_specs=[
            pl.BlockSpec((bm, d), lambda i, j: (i, 0), pipeline_mode=pl.Buffered(1)),
            pl.BlockSpec((1, d), lambda i, j: (0, 0)),
            pl.BlockSpec((pl.Element(d), pl.Element(2 * bf)), lambda i, j: (0, off(j))),
            pl.BlockSpec((pl.Element(d), pl.Element(2 * bf)), lambda i, j: (0, off(j))),
            pl.BlockSpec((pl.Element(2 * bf), pl.Element(d)), lambda i, j: (off(j), 0)),
            pl.BlockSpec((1, d), lambda i, j: (0, 0)),
        ],
        out_specs=pl.BlockSpec((bm, d), lambda i, j: (i, 0), pipeline_mode=pl.Buffered(1)),
        out_shape=jax.ShapeDtypeStruct((m, d), F32),
        scratch_shapes=[pltpu.VMEM((bm, d), BF16)],
        compiler_params=_params("parallel", "arbitrary"),
        name="ffn",
    )(h, gpre, wg, wu, wd, gpost)


def _ffn_convert_kernel(h_ref, gpre_ref, wg_ref, wu_ref, wd_ref, gpost_ref, o_ref, wgb_ref, wub_ref, wdb_ref, u_scr,
                        *, col_chunk):
    j = pl.program_id(0)

    @pl.when(j == 0)
    def _():
        _prenorm_rows(h_ref, gpre_ref, u_scr)
        o_ref[...] = jnp.zeros_like(o_ref)

    wgb_ref[...] = wg_ref[...].astype(BF16)
    wub_ref[...] = wu_ref[...].astype(BF16)
    wdb_ref[...] = wd_ref[...].astype(BF16)
    _swiglu_accumulate(u_scr[...], wgb_ref[...], wub_ref[...], wdb_ref, slice(None), o_ref, col_chunk, False)

    @pl.when(j == pl.num_programs(0) - 1)
    def _():
        _postnorm_rows(h_ref, gpost_ref, o_ref, 0.5)


def _ffn_convert(h, gpre, wg, wu, wd, gpost, *, bf_want=256):
    m, d = h.shape
    f = wg.shape[1]
    bf = _tile(f, bf_want)
    cols = pl.BlockSpec((d, bf), lambda j: (0, j))
    rows = pl.BlockSpec((bf, d), lambda j: (j, 0))
    full = pl.BlockSpec((m, d), lambda j: (0, 0))
    vec = pl.BlockSpec((1, d), lambda j: (0, 0))
    res = pl.pallas_call(
        functools.partial(_ffn_convert_kernel, col_chunk=_tile(d, 512)),
        grid=(f // bf,),
        in_specs=[full, vec, cols, cols, rows, vec],
        out_specs=[full, cols, cols, rows],
        out_shape=[jax.ShapeDtypeStruct((m, d), F32)] + [jax.ShapeDtypeStruct(w.shape, BF16) for w in (wg, wu, wd)],
        scratch_shapes=[pltpu.VMEM((m, d), BF16)],
        compiler_params=_params("arbitrary"),
        name="ffn_convert",
    )(h, gpre, wg, wu, wd, gpost)
    return res[0], tuple(res[1:])


def _inproj_kernel(*refs, emit, nblks):
    h_ref, g_ref, w_ref = refs[:3]
    ns = len(nblks)
    side_in = refs[3:3 + ns]
    o_ref = refs[3 + ns]
    side_out = refs[len(refs) - 1 - ns:len(refs) - 1]
    u_scr = refs[-1]

    @pl.when(pl.program_id(1) == 0)
    def _():
        _prenorm_rows(h_ref, g_ref, u_scr)

    _convert_side(pl.program_id(0) * pl.num_programs(1) + pl.program_id(1), side_in, side_out, nblks)
    if emit:
        wb_ref = refs[4 + ns]
        wb_ref[...] = w_ref[...].astype(BF16)
        w_ref = wb_ref
    o_ref[...] = jnp.dot(u_scr[...], w_ref[...], preferred_element_type=F32)


def _inproj(h, gain, w, side=(), *, bm_want=1024, bn_want=512):
    m, d = h.shape
    n = w.shape[1]
    bm = _tile(m, bm_want)
    bn = _tile(n, bn_want)
    nj = n // bn
    emit = w.dtype != BF16
    out_specs = [pl.BlockSpec((bm, bn), lambda i, j: (i, j))]
    out_shape = [jax.ShapeDtypeStruct((m, n), F32)]
    if emit:
        assert m == bm, "weights are converted on a single pass over the column tiles"
        out_specs.append(pl.BlockSpec((d, bn), lambda i, j: (0, j)))
        out_shape.append(jax.ShapeDtypeStruct(w.shape, BF16))
    side_specs, nblks = _side_specs(side, (m // bm) * nj, lambda i, j: i * nj + j)
    res = pl.pallas_call(
        functools.partial(_inproj_kernel, emit=emit, nblks=tuple(nblks)),
        grid=(m // bm, nj),
        in_specs=[
            pl.BlockSpec((bm, d), lambda i, j: (i, 0), pipeline_mode=pl.Buffered(1)),
            pl.BlockSpec((1, d), lambda i, j: (0, 0)),
            pl.BlockSpec((d, bn), lambda i, j: (0, j)),
        ] + side_specs,
        out_specs=out_specs + side_specs,
        out_shape=out_shape + _side_shapes(side),
        scratch_shapes=[pltpu.VMEM((bm, d), BF16)],
        compiler_params=_params("arbitrary", "arbitrary"),
        name="inproj_convert" if emit else "inproj",
    )(h, gain, w, *side)
    return res[0], (res[1] if emit else w), tuple(res[len(res) - len(side):]) if side else ()


def _outproj_kernel(yh_ref, yl_ref, w_ref, h_ref, g_ref, o_ref, *, col_chunk):
    kh = yh_ref.shape[1]
    yh = yh_ref[...]
    yl = yl_ref[...]
    for c0 in range(0, o_ref.shape[1], col_chunk):
        o_ref[:, c0:c0 + col_chunk] = (
            jnp.dot(yh, w_ref[0:kh, c0:c0 + col_chunk], preferred_element_type=F32)
            + jnp.dot(yl, w_ref[kh:2 * kh, c0:c0 + col_chunk], preferred_element_type=F32))
    _postnorm_rows(h_ref, g_ref, o_ref, 1.0)


def _outproj(yh, yl, w, h, gain, *, bm_want=128):
    m, kh = yh.shape
    d = w.shape[1]
    bm = _tile(m, bm_want)
    return pl.pallas_call(
        functools.partial(_outproj_kernel, col_chunk=_tile(d, 1024)),
        grid=(m // bm,),
        in_specs=[
            pl.BlockSpec((bm, kh), lambda i: (i, 0)),
            pl.BlockSpec((bm, kh), lambda i: (i, 0)),
            pl.BlockSpec((2 * kh, d), lambda i: (0, 0), pipeline_mode=pl.Buffered(1)),
            pl.BlockSpec((bm, d), lambda i: (i, 0)),
            pl.BlockSpec((1, d), lambda i: (0, 0)),
        ],
        out_specs=pl.BlockSpec((bm, d), lambda i: (i, 0)),
        out_shape=jax.ShapeDtypeStruct((m, d), F32),
        compiler_params=_params("parallel"),
        name="outproj",
    )(yh, yl, w, h, gain)


def _cumsum_rows(x):
    r, w = x.shape
    row = lax.broadcasted_iota(jnp.int32, (SUBLANES, w), 0)
    out = []
    carry = None
    for g in range(r // SUBLANES):
        v = x[g * SUBLANES:(g + 1) * SUBLANES]
        for dd in (1, 2, 4):
            v = v + jnp.where(row >= dd, pltpu.roll(v, dd, 0), 0.0)
        if carry is not None:
            v = v + carry
        carry = jnp.broadcast_to(v[SUBLANES - 1:SUBLANES], (SUBLANES, w))
        out.append(v)
    return jnp.concatenate(out, axis=0)


def _hgrn_gates(q_raw, f_raw, lb):
    q = q_raw * _sigmoid(q_raw)
    sf = _sigmoid(f_raw)
    fg = lb + (1.0 - lb) * sf
    k = (1.0 - lb) * (1.0 - sf)
    return q, k, jnp.log(fg) * LOG2E, fg


def _level_reference(cum, s):
    r, w = cum.shape
    if s >= SUBLANES:
        parts = [jnp.broadcast_to(cum[p + s - 1:p + s], (2 * s, w)) for p in range(0, r, 2 * s)]
        return jnp.concatenate(parts, axis=0)
    row = lax.broadcasted_iota(jnp.int32, (SUBLANES, w), 0)
    parts = []
    for g in range(r // SUBLANES):
        v = cum[g * SUBLANES:(g + 1) * SUBLANES]
        if s == 4:
            ref = jnp.broadcast_to(v[3:4], (SUBLANES, w))
        elif s == 2:
            ref = jnp.where(row < 4, jnp.broadcast_to(v[1:2], (SUBLANES, w)),
                            jnp.broadcast_to(v[5:6], (SUBLANES, w)))
        else:
            ref = jnp.where((row & 1) == 1, pltpu.roll(v, 1, 0), v)
        parts.append(ref)
    return jnp.concatenate(parts, axis=0)


_NT = (((1,), (1,)), ((), ()))
_TN = (((0,), (0,)), ((), ()))


def _hgrn_unit(q_ref, f_ref, i_ref, g_ref, lb_ref, gain_ref, o_ref, st_scr, *, hb, tb):
    w = hb * HEAD
    lb = lb_ref[...]
    gain = gain_ref[...]
    ri = lax.broadcasted_iota(jnp.int32, (CHUNK, CHUNK), 0)
    ci = lax.broadcasted_iota(jnp.int32, (CHUNK, CHUNK), 1)
    roww = lax.broadcasted_iota(jnp.int32, (CHUNK, w), 0)
    levels = (32, 16, 8, 4, 2, 1)
    keeps = [(ri == ci).astype(F32)] + [
        None if 2 * s == CHUNK else ((ri & -(2 * s)) == (ci & -(2 * s))).astype(F32) for s in levels]
    second = [(roww & s) != 0 for s in levels]

    heads = [slice(h * HEAD, (h + 1) * HEAD) for h in range(hb)]

    prep = []
    for c in range(tb // CHUNK):
        rows = slice(c * CHUNK, (c + 1) * CHUNK)
        q, k, lf, fg = _hgrn_gates(q_ref[rows, :], f_ref[rows, :], lb)
        cum = _cumsum_rows(lf)
        last = jnp.broadcast_to(cum[CHUNK - 1:CHUNK], (CHUNK, w))
        q_in = (q * jnp.exp2(cum)).astype(BF16)
        k_out = (k * jnp.exp2(last - cum)).astype(BF16)
        qs, ks = [q.astype(BF16)], [k.astype(BF16)]
        for s, m in zip(levels, second):
            if s == 1:
                qs.append(jnp.where(m, q * fg, 0.0).astype(BF16))
                ks.append(jnp.where(m, 0.0, k).astype(BF16))
                continue
            ref = _level_reference(cum, s)
            e = jnp.exp2(jnp.where(m, cum - ref, ref - cum))
            qk = jnp.where(m, q, k) * e
            qs.append(jnp.where(m, qk, 0.0).astype(BF16))
            ks.append(jnp.where(m, 0.0, qk).astype(BF16))
        prep.append((q_in, k_out, i_ref[rows, :].astype(BF16), qs, ks, last))

    scores = []
    for _, _, _, qs, ks, _ in prep:
        per_head = []
        for sl in heads:
            a = None
            for keep, qt, kt in zip(keeps, qs, ks):
                p = lax.dot_general(qt[:, sl], kt[:, sl], _NT, preferred_element_type=F32)
                if keep is not None:
                    p = p * keep
                a = p if a is None else a + p
            per_head.append(a.astype(BF16))
        scores.append(per_head)

    state = [st_scr[h] for h in range(hb)]
    for c, (q_in, k_out, vb, _, _, last) in enumerate(prep):
        rows = slice(c * CHUNK, (c + 1) * CHUNK)
        outs = []
        for h, sl in enumerate(heads):
            o = lax.dot_general(q_in[:, sl], state[h].astype(BF16), _NT, preferred_element_type=F32)
            o = o + jnp.dot(scores[c][h], vb[:, sl], preferred_element_type=F32)
            state[h] = state[h] * jnp.exp2(last[0:1, sl]) + lax.dot_general(
                vb[:, sl], k_out[:, sl], _TN, preferred_element_type=F32)
            outs.append(o * lax.rsqrt(jnp.mean(o * o, axis=-1, keepdims=True) + EPS))
        graw = g_ref[rows, :]
        y = jnp.concatenate(outs, axis=1) * gain * (graw * _sigmoid(graw))
        o_ref[rows, :] = y.astype(o_ref.dtype)
    for h in range(hb):
        st_scr[h] = state[h]


def _hgrn_kernel(q_ref, f_ref, i_ref, g_ref, lb_ref, gain_ref, s0_ref, o_ref, st_scr, *, hb, tb):
    @pl.when(pl.program_id(2) == 0)
    def _():
        st_scr[...] = s0_ref[...]

    _hgrn_unit(q_ref, f_ref, i_ref, g_ref, lb_ref, gain_ref, o_ref, st_scr, hb=hb, tb=tb)


def _hgrn(proj, lb, gain, s0, *, nb, hw, hb_want=4, tb_want=256):
    m = proj.shape[0]
    t = m // nb
    nh = hw // HEAD
    hb = _tile(nh, hb_want)
    w = hb * HEAD
    tb = _tile(t, tb_want)
    nt = t // tb
    ng = nh // hb

    def col(group):
        return lambda b, hg, tt: (b * nt + tt, group * ng + hg)

    return pl.pallas_call(
        functools.partial(_hgrn_kernel, hb=hb, tb=tb),
        grid=(nb, ng, nt),
        in_specs=[
            pl.BlockSpec((tb, w), col(0)),
            pl.BlockSpec((tb, w), col(1)),
            pl.BlockSpec((tb, w), col(2)),
            pl.BlockSpec((tb, w), col(3)),
            pl.BlockSpec((1, w), lambda b, hg, tt: (0, hg)),
            pl.BlockSpec((1, w), lambda b, hg, tt: (0, hg)),
            pl.BlockSpec((hb, HEAD, HEAD), lambda b, hg, tt: (hg, 0, 0)),
        ],
        out_specs=pl.BlockSpec((tb, w), lambda b, hg, tt: (b * nt + tt, hg)),
        out_shape=jax.ShapeDtypeStruct((m, hw), BF16),
        scratch_shapes=[pltpu.VMEM((hb, HEAD, HEAD), F32)],
        compiler_params=_params("parallel", "parallel", "arbitrary"),
        name="hgrn",
    )(proj, proj, proj, proj, lb, gain, s0)


def _hgrn_meta_kernel(f_ref, i_ref, lb_ref, s_ref, *, nh):
    f_raw = f_ref[...]
    _, k, lf, _ = _hgrn_gates(f_raw, f_raw, lb_ref[...])
    cum = _cumsum_rows(lf)
    r, w = cum.shape
    last = jnp.broadcast_to(cum[r - 1:r], (r, w))
    k_out = (k * jnp.exp2(last - cum)).astype(BF16)
    vb = i_ref[...].astype(BF16)
    for h in range(nh):
        sl = slice(h * HEAD, (h + 1) * HEAD)
        s_ref[h] = lax.dot_general(vb[:, sl], k_out[:, sl], _TN, preferred_element_type=F32)


def _hgrn_meta(proj_meta, lb, *, hw):
    nmeta = proj_meta.shape[0]
    nh = hw // HEAD
    return pl.pallas_call(
        functools.partial(_hgrn_meta_kernel, nh=nh),
        grid=(1,),
        in_specs=[
            pl.BlockSpec((nmeta, hw), lambda i: (0, 1)),
            pl.BlockSpec((nmeta, hw), lambda i: (0, 2)),
            pl.BlockSpec((1, hw), lambda i: (0, 0)),
        ],
        out_specs=pl.BlockSpec((nh, HEAD, HEAD), lambda i: (0, 0, 0)),
        out_shape=jax.ShapeDtypeStruct((nh, HEAD, HEAD), F32),
        compiler_params=_params("arbitrary"),
        name="hgrn_meta",
    )(proj_meta, proj_meta, lb)


def _lru_unit(x_ref, g_ref, cw_ref, cb_ref, wa_ref, ba_ref, wx_ref, bx_ref, lam_ref, gain_ref, y_ref,
              xbuf, a_scr, b_scr, hcar, *, tb, nblk, blk):
    c = x_ref.shape[1]
    nlam = -lam_ref[...]
    sp = jnp.maximum(nlam, 0.0) + jnp.log(1.0 + jnp.exp(-jnp.abs(nlam)))
    rate = (-RG_C * LOG2E) * sp
    for hh in range(nblk):
        sl = slice(hh * blk, (hh + 1) * blk)
        x = x_ref[:, sl]
        xbuf[SUBLANES:SUBLANES + tb, sl] = x
        xc = (cb_ref[:, sl] + cw_ref[3:4, sl] * x
              + cw_ref[2:3, sl] * xbuf[SUBLANES - 1:SUBLANES - 1 + tb, sl]
              + cw_ref[1:2, sl] * xbuf[SUBLANES - 2:SUBLANES - 2 + tb, sl]
              + cw_ref[0:1, sl] * xbuf[SUBLANES - 3:SUBLANES - 3 + tb, sl])
        xbuf[0:SUBLANES, sl] = x[tb - SUBLANES:tb, :]
        xcb = xc.astype(BF16)
        r = _sigmoid(jnp.dot(xcb, wa_ref[hh], preferred_element_type=F32) + ba_ref[:, sl])
        ig = _sigmoid(jnp.dot(xcb, wx_ref[hh], preferred_element_type=F32) + bx_ref[:, sl])
        a = jnp.exp2(r * rate[:, sl])
        a_scr[:, sl] = a
        om = 1.0 - a * a
        root = jnp.where(om > 0.0, om * lax.rsqrt(om), 0.0)
        b_scr[:, sl] = root * (ig * xc)

    row = lax.broadcasted_iota(jnp.int32, (SUBLANES, c), 0)

    def group(gi, hprev):
        r0 = pl.multiple_of(gi * SUBLANES, SUBLANES)
        av = a_scr[pl.ds(r0, SUBLANES), :]
        bv = b_scr[pl.ds(r0, SUBLANES), :]
        for dd in (1, 2, 4):
            keep = row >= dd
            bv = jnp.where(keep, av * pltpu.roll(bv, dd, 0) + bv, bv)
            av = jnp.where(keep, av * pltpu.roll(av, dd, 0), av)
        hv = av * hprev + bv
        b_scr[pl.ds(r0, SUBLANES), :] = hv
        return jnp.broadcast_to(hv[SUBLANES - 1:SUBLANES], (SUBLANES, c))

    hlast = lax.fori_loop(0, tb // SUBLANES, group, hcar[...])
    hcar[...] = hlast

    blocks = [slice(hh * blk, (hh + 1) * blk) for hh in range(nblk)]
    ssq = sum(jnp.sum(b_scr[:, sl] * b_scr[:, sl], axis=-1, keepdims=True) for sl in blocks)
    rstd = lax.rsqrt(ssq * (1.0 / c) + EPS)
    for sl in blocks:
        gr = g_ref[:, sl]
        half = 0.5 * gr
        gelu = half + half * jnp.tanh(gr * (GELU_K + (GELU_K * 0.044715) * (gr * gr)))
        y_ref[:, sl] = (b_scr[:, sl] * rstd * gain_ref[:, sl] * gelu).astype(y_ref.dtype)
    return hlast


def _lru_kernel(*refs, tb, nblk, blk, nblks):
    (x_ref, g_ref, cw_ref, cb_ref, wa_ref, ba_ref, wx_ref, bx_ref, lam_ref, gain_ref, h0_ref, xt0_ref) = refs[:12]
    ns = len(nblks)
    side_in = refs[12:12 + ns]
    y_ref, hl_ref = refs[12 + ns:14 + ns]
    side_out = refs[14 + ns:14 + 2 * ns]
    xbuf, a_scr, b_scr, hcar = refs[-4:]
    t = pl.program_id(1)

    @pl.when(t == 0)
    def _():
        hcar[...] = h0_ref[...]
        xbuf[0:SUBLANES, :] = xt0_ref[...]

    _convert_side(pl.program_id(0) * pl.num_programs(1) + t, side_in, side_out, nblks)
    hlast = _lru_unit(x_ref, g_ref, cw_ref, cb_ref, wa_ref, ba_ref, wx_ref, bx_ref, lam_ref, gain_ref, y_ref,
                      xbuf, a_scr, b_scr, hcar, tb=tb, nblk=nblk, blk=blk)

    @pl.when(t == pl.num_programs(1) - 1)
    def _():
        hl_ref[0] = hlast


def _lru(proj, cw, cb, wa, ba, wx, bx, lam, gain, h0, xt0, side=(), *, nb, lw, xcol, tb_want=256):
    m = proj.shape[0]
    t = m // nb
    tb = _tile(t, tb_want)
    nt = t // tb
    nblk, blk, _ = wa.shape
    vec = pl.BlockSpec((1, lw), lambda b, tt: (0, 0))
    side_specs, nblks = _side_specs(side, nb * nt, lambda b, tt: b * nt + tt)
    res = pl.pallas_call(
        functools.partial(_lru_kernel, tb=tb, nblk=nblk, blk=blk, nblks=tuple(nblks)),
        grid=(nb, nt),
        in_specs=[
            pl.BlockSpec((tb, lw), lambda b, tt: (b * nt + tt, xcol)),
            pl.BlockSpec((tb, lw), lambda b, tt: (b * nt + tt, xcol + 1)),
            pl.BlockSpec((cw.shape[0], lw), lambda b, tt: (0, 0)),
            vec,
            pl.BlockSpec((nblk, blk, blk), lambda b, tt: (0, 0, 0)),
            vec,
            pl.BlockSpec((nblk, blk, blk), lambda b, tt: (0, 0, 0)),
            vec, vec, vec,
            pl.BlockSpec((SUBLANES, lw), lambda b, tt: (0, 0)),
            pl.BlockSpec((SUBLANES, lw), lambda b, tt: (0, 0)),
        ] + side_specs,
        out_specs=[
            pl.BlockSpec((tb, lw), lambda b, tt: (b * nt + tt, 0)),
            pl.BlockSpec((1, SUBLANES, lw), lambda b, tt: (b, 0, 0)),
        ] + side_specs,
        out_shape=[
            jax.ShapeDtypeStruct((m, lw), BF16),
            jax.ShapeDtypeStruct((nb, SUBLANES, lw), F32),
        ] + _side_shapes(side),
        scratch_shapes=[
            pltpu.VMEM((SUBLANES + tb, lw), F32),
            pltpu.VMEM((tb, lw), F32),
            pltpu.VMEM((tb, lw), F32),
            pltpu.VMEM((SUBLANES, lw), F32),
        ],
        compiler_params=_params("arbitrary", "arbitrary"),
        name="lru",
    )(proj, proj, cw, cb, wa, ba, wx, bx, lam, gain, h0, xt0, *side)
    return res[0], res[1], tuple(res[2:])


def kernel(x, meta_tokens, ffn1_pre_norm, ffn1_w_gate, ffn1_w_up, ffn1_w_down, ffn1_post_norm, mix_pre_norm, w_in, hgrn_lb_logits, hgrn_out_norm, lru_conv_w, lru_conv_b, lru_w_a, lru_b_a, lru_w_x, lru_b_x, lru_lambda, lru_out_norm, w_out, mix_post_norm, ffn2_pre_norm, ffn2_w_gate, ffn2_w_up, ffn2_w_down, ffn2_post_norm):
    nb, seq, d = x.shape
    hw = hgrn_out_norm.shape[1]
    lw = lru_out_norm.shape[1]
    assert ffn1_pre_norm.shape[0] == 1, "one layer"
    assert hw == lw and w_in.shape[2] == 4 * hw + 2 * lw and hw % HEAD == 0
    assert seq % CHUNK == 0 and meta_tokens.shape[0] % SUBLANES == 0

    def vec(p):
        return p[0].reshape(1, -1).astype(F32)

    lb = jnp.cumsum(jax.nn.softmax(hgrn_lb_logits.astype(F32), axis=0), axis=0)[0].reshape(1, hw)
    lru_p = (lru_conv_w[0].astype(F32), vec(lru_conv_b), lru_w_a[0].astype(BF16), vec(lru_b_a),
             lru_w_x[0].astype(BF16), vec(lru_b_x), vec(lru_lambda), vec(lru_out_norm))
    xcol = (4 * hw) // lw

    hm, ffn1_w = _ffn_convert(meta_tokens.astype(F32), vec(ffn1_pre_norm), ffn1_w_gate[0], ffn1_w_up[0],
                              ffn1_w_down[0], vec(ffn1_post_norm))
    pm, w_in_b, _ = _inproj(hm, vec(mix_pre_norm), w_in[0], bn_want=512)
    s_meta = _hgrn_meta(pm, lb, hw=hw)
    zeros8 = jnp.zeros((SUBLANES, lw), F32)
    _, h_meta, _ = _lru(pm, *lru_p, zeros8, zeros8, nb=1, lw=lw, xcol=xcol)
    xt_meta = pm[pm.shape[0] - SUBLANES:, 4 * hw:4 * hw + lw]

    h = x.reshape(nb * seq, d).astype(F32)
    h = _ffn(h, vec(ffn1_pre_norm), *ffn1_w, vec(ffn1_post_norm))
    proj, _, ffn2_w = _inproj(h, vec(mix_pre_norm), w_in_b, (ffn2_w_gate[0], ffn2_w_up[0], ffn2_w_down[0]))
    y_h = _hgrn(proj, lb, vec(hgrn_out_norm), s_meta, nb=nb, hw=hw)
    y_l, _, (w_out_b,) = _lru(proj, *lru_p, h_meta[0], xt_meta, (w_out[0],), nb=nb, lw=lw, xcol=xcol)
    h = _outproj(y_h, y_l, w_out_b, h, vec(mix_post_norm))
    h = _ffn(h, vec(ffn2_pre_norm), *ffn2_w, vec(ffn2_post_norm))
    return h.reshape(nb, seq, d).astype(x.dtype)
```

```python
import functools

import jax
import jax.numpy as jnp
from jax import lax
from jax.experimental import pallas as pl
from jax.experimental.pallas import tpu as pltpu

F32 = jnp.float32
BF16 = jnp.bfloat16

EPS = 1e-6
RG_C = 8.0
LOG2E = 1.4426950408889634
GELU_K = 0.7978845608028654
HEAD = 128
CHUNK = 64
SUBLANES = 8
BF16_ROWS = 16
V7X_VMEM_LIMIT_BYTES = 60 * 1024 * 1024


def _tile(n, want):
    t = min(n, want)
    while n % t:
        t -= 1
    return t


def _sigmoid(x):
    return 0.5 * jnp.tanh(0.5 * x) + 0.5


def _rms(x, gain):
    return x * lax.rsqrt(jnp.mean(x * x, axis=-1, keepdims=True) + EPS) * gain


PRENORM_ROWS = 256
POSTNORM_ROWS = SUBLANES


def _prenorm_rows(h_ref, gain_ref, u_ref):
    step = _tile(h_ref.shape[0], PRENORM_ROWS)
    for r0 in range(0, h_ref.shape[0], step):
        u_ref[r0:r0 + step, :] = _rms(h_ref[r0:r0 + step, :], gain_ref[...]).astype(u_ref.dtype)


def _postnorm_rows(h_ref, gain_ref, o_ref, scale):
    step = _tile(o_ref.shape[0], POSTNORM_ROWS)
    for r0 in range(0, o_ref.shape[0], step):
        o_ref[r0:r0 + step, :] = h_ref[r0:r0 + step, :] + scale * _rms(o_ref[r0:r0 + step, :], gain_ref[...])


def _params(*sem):
    return pltpu.CompilerParams(dimension_semantics=sem, vmem_limit_bytes=V7X_VMEM_LIMIT_BYTES)


def _side_specs(side, n_steps, linear):
    specs, nblks = [], []
    for arr in side:
        rows, cols = arr.shape
        assert rows % BF16_ROWS == 0
        nblk = max(k for k in range(1, n_steps + 1) if (rows // BF16_ROWS) % k == 0)
        specs.append(pl.BlockSpec((rows // nblk, cols),
                                  lambda *ids, _n=nblk: (jnp.minimum(linear(*ids), _n - 1), 0)))
        nblks.append(nblk)
    return specs, nblks


def _convert_side(step, side_in, side_out, nblks):
    for src, dst, nblk in zip(side_in, side_out, nblks):
        @pl.when(step < nblk)
        def _():
            dst[...] = src[...].astype(dst.dtype)


def _side_shapes(side):
    return [jax.ShapeDtypeStruct(arr.shape, BF16) for arr in side]


def _swiglu_accumulate(u, wg, wu, wd_ref, rows, o_ref, col_chunk, assign):
    g = jnp.dot(u, wg, preferred_element_type=F32)
    v = jnp.dot(u, wu, preferred_element_type=F32)
    a = (g * jax.nn.sigmoid(g) * v).astype(BF16)
    for c0 in range(0, o_ref.shape[1], col_chunk):
        part = jnp.dot(a, wd_ref[rows, c0:c0 + col_chunk], preferred_element_type=F32)
        if assign:
            o_ref[:, c0:c0 + col_chunk] = part
        else:
            o_ref[:, c0:c0 + col_chunk] += part


def _ffn_kernel(h_ref, gpre_ref, wg_ref, wu_ref, wd_ref, gpost_ref, o_ref, u_scr, *, first, col_chunk):
    j = pl.program_id(1)
    win = wg_ref.shape[1]
    nf = first * (win // 2)

    @pl.when(j == 0)
    def _():
        _prenorm_rows(h_ref, gpre_ref, u_scr)
        _swiglu_accumulate(u_scr[...], wg_ref[:, 0:nf], wu_ref[:, 0:nf], wd_ref, slice(0, nf), o_ref,
                           col_chunk, True)

    @pl.when(j > 0)
    def _():
        _swiglu_accumulate(u_scr[...], wg_ref[...], wu_ref[...], wd_ref, slice(0, win), o_ref, col_chunk, False)

    @pl.when(j == pl.num_programs(1) - 1)
    def _():
        _postnorm_rows(h_ref, gpost_ref, o_ref, 0.5)


def _ffn(h, gpre, wg, wu, wd, gpost, *, bm_want=512, bf_want=256):
    m, d = h.shape
    f = wg.shape[1]
    bm = _tile(m, bm_want)
    bf = _tile(f, bf_want)
    nt = f // bf
    assert nt >= 2 and wg.dtype == BF16
    first = 1 if nt % 2 else 2
    nj = 1 + (nt - first) // 2

    def off(j):
        return jnp.maximum(first + 2 * (j - 1), 0) * bf

    return pl.pallas_call(
        functools.partial(_ffn_kernel, first=first, col_chunk=_tile(d, 512)),
        grid=(m // bm, nj),
        in_specs=[
            pl.BlockSpec((bm, d), lambda i, j: (i, 0), pipeline_mode=pl.Buffered(1)),
            pl.BlockSpec((1, d), lambda i, j: (0, 0)),
            pl.BlockSpec((pl.Element(d), pl.Element(2 * bf)), lambda i, j: (0, off(j))),
            pl.BlockSpec((pl.Element(d), pl.Element(2 * bf)), lambda i, j: (0, off(j))),
            pl.BlockSpec((pl.Element(2 * bf), pl.Element(d)), lambda i, j: (off(j), 0)),
            pl.BlockSpec((1, d), lambda i, j: (0, 0)),
        ],
        out_specs=pl.BlockSpec((bm, d), lambda i, j: (i, 0), pipeline_mode=pl.Buffered(1)),
        out_shape=jax.ShapeDtypeStruct((m, d), F32),
        scratch_shapes=[pltpu.VMEM((bm, d), BF16)],
        compiler_params=_params("parallel", "arbitrary"),
        name="ffn",
    )(h, gpre, wg, wu, wd, gpost)


def _ffn_convert_kernel(h_ref, gpre_ref, wg_ref, wu_ref, wd_ref, gpost_ref, o_ref, wgb_ref, wub_ref, wdb_ref, u_scr,
                        *, col_chunk):
    j = pl.program_id(0)

    @pl.when(j == 0)
    def _():
        _prenorm_rows(h_ref, gpre_ref, u_scr)
        o_ref[...] = jnp.zeros_like(o_ref)

    wgb_ref[...] = wg_ref[...].astype(BF16)
    wub_ref[...] = wu_ref[...].astype(BF16)
    wdb_ref[...] = wd_ref[...].astype(BF16)
    _swiglu_accumulate(u_scr[...], wgb_ref[...], wub_ref[...], wdb_ref, slice(None), o_ref, col_chunk, False)

    @pl.when(j == pl.num_programs(0) - 1)
    def _():
        _postnorm_rows(h_ref, gpost_ref, o_ref, 0.5)


def _ffn_convert(h, gpre, wg, wu, wd, gpost, *, bf_want=256):
    m, d = h.shape
    f = wg.shape[1]
    bf = _tile(f, bf_want)
    cols = pl.BlockSpec((d, bf), lambda j: (0, j))
    rows = pl.BlockSpec((bf, d), lambda j: (j, 0))
    full = pl.BlockSpec((m, d), lambda j: (0, 0))
    vec = pl.BlockSpec((1, d), lambda j: (0, 0))
    res = pl.pallas_call(
        functools.partial(_ffn_convert_kernel, col_chunk=_tile(d, 512)),
        grid=(f // bf,),
        in_specs=[full, vec, cols, cols, rows, vec],
        out_specs=[full, cols, cols, rows],
        out_shape=[jax.ShapeDtypeStruct((m, d), F32)] + [jax.ShapeDtypeStruct(w.shape, BF16) for w in (wg, wu, wd)],
        scratch_shapes=[pltpu.VMEM((m, d), BF16)],
        compiler_params=_params("arbitrary"),
        name="ffn_convert",
    )(h, gpre, wg, wu, wd, gpost)
    return res[0], tuple(res[1:])


def _inproj_kernel(*refs, emit, nblks):
    h_ref, g_ref, w_ref = refs[:3]
    ns = len(nblks)
    side_in = refs[3:3 + ns]
    o_ref = refs[3 + ns]
    side_out = refs[len(refs) - 1 - ns:len(refs) - 1]
    u_scr = refs[-1]

    @pl.when(pl.program_id(1) == 0)
    def _():
        _prenorm_rows(h_ref, g_ref, u_scr)

    _convert_side(pl.program_id(0) * pl.num_programs(1) + pl.program_id(1), side_in, side_out, nblks)
    if emit:
        wb_ref = refs[4 + ns]
        wb_ref[...] = w_ref[...].astype(BF16)
        w_ref = wb_ref
    o_ref[...] = jnp.dot(u_scr[...], w_ref[...], preferred_element_type=F32)


def _inproj(h, gain, w, side=(), *, bm_want=1024, bn_want=512):
    m, d = h.shape
    n = w.shape[1]
    bm = _tile(m, bm_want)
    bn = _tile(n, bn_want)
    nj = n // bn
    emit = w.dtype != BF16
    out_specs = [pl.BlockSpec((bm, bn), lambda i, j: (i, j))]
    out_shape = [jax.ShapeDtypeStruct((m, n), F32)]
    if emit:
        assert m == bm, "weights are converted on a single pass over the column tiles"
        out_specs.append(pl.BlockSpec((d, bn), lambda i, j: (0, j)))
        out_shape.append(jax.ShapeDtypeStruct(w.shape, BF16))
    side_specs, nblks = _side_specs(side, (m // bm) * nj, lambda i, j: i * nj + j)
    res = pl.pallas_call(
        functools.partial(_inproj_kernel, emit=emit, nblks=tuple(nblks)),
        grid=(m // bm, nj),
        in_specs=[
            pl.BlockSpec((bm, d), lambda i, j: (i, 0), pipeline_mode=pl.Buffered(1)),
            pl.BlockSpec((1, d), lambda i, j: (0, 0)),
            pl.BlockSpec((d, bn), lambda i, j: (0, j)),
        ] + side_specs,
        out_specs=out_specs + side_specs,
        out_shape=out_shape + _side_shapes(side),
        scratch_shapes=[pltpu.VMEM((bm, d), BF16)],
        compiler_params=_params("arbitrary", "arbitrary"),
        name="inproj_convert" if emit else "inproj",
    )(h, gain, w, *side)
    return res[0], (res[1] if emit else w), tuple(res[len(res) - len(side):]) if side else ()


def _outproj_kernel(yh_ref, yl_ref, w_ref, h_ref, g_ref, o_ref, *, col_chunk):
    kh = yh_ref.shape[1]
    yh = yh_ref[...]
    yl = yl_ref[...]
    for c0 in range(0, o_ref.shape[1], col_chunk):
        o_ref[:, c0:c0 + col_chunk] = (
            jnp.dot(yh, w_ref[0:kh, c0:c0 + col_chunk], preferred_element_type=F32)
            + jnp.dot(yl, w_ref[kh:2 * kh, c0:c0 + col_chunk], preferred_element_type=F32))
    _postnorm_rows(h_ref, g_ref, o_ref, 1.0)


def _outproj(yh, yl, w, h, gain, *, bm_want=128):
    m, kh = yh.shape
    d = w.shape[1]
    bm = _tile(m, bm_want)
    return pl.pallas_call(
        functools.partial(_outproj_kernel, col_chunk=_tile(d, 1024)),
        grid=(m // bm,),
        in_specs=[
            pl.BlockSpec((bm, kh), lambda i: (i, 0)),
            pl.BlockSpec((bm, kh), lambda i: (i, 0)),
            pl.BlockSpec((2 * kh, d), lambda i: (0, 0), pipeline_mode=pl.Buffered(1)),
            pl.BlockSpec((bm, d), lambda i: (i, 0)),
            pl.BlockSpec((1, d), lambda i: (0, 0)),
        ],
        out_specs=pl.BlockSpec((bm, d), lambda i: (i, 0)),
        out_shape=jax.ShapeDtypeStruct((m, d), F32),
        compiler_params=_params("parallel"),
        name="outproj",
    )(yh, yl, w, h, gain)


def _cumsum_rows(x):
    r, w = x.shape
    row = lax.broadcasted_iota(jnp.int32, (SUBLANES, w), 0)
    out = []
    carry = None
    for g in range(r // SUBLANES):
        v = x[g * SUBLANES:(g + 1) * SUBLANES]
        for dd in (1, 2, 4):
            v = v + jnp.where(row >= dd, pltpu.roll(v, dd, 0), 0.0)
        if carry is not None:
            v = v + carry
        carry = jnp.broadcast_to(v[SUBLANES - 1:SUBLANES], (SUBLANES, w))
        out.append(v)
    return jnp.concatenate(out, axis=0)


def _hgrn_gates(q_raw, f_raw, lb):
    q = q_raw * _sigmoid(q_raw)
    sf = _sigmoid(f_raw)
    fg = lb + (1.0 - lb) * sf
    k = (1.0 - lb) * (1.0 - sf)
    return q, k, jnp.log(fg) * LOG2E, fg


def _level_reference(cum, s):
    r, w = cum.shape
    if s >= SUBLANES:
        parts = [jnp.broadcast_to(cum[p + s - 1:p + s], (2 * s, w)) for p in range(0, r, 2 * s)]
        return jnp.concatenate(parts, axis=0)
    row = lax.broadcasted_iota(jnp.int32, (SUBLANES, w), 0)
    parts = []
    for g in range(r // SUBLANES):
        v = cum[g * SUBLANES:(g + 1) * SUBLANES]
        if s == 4:
            ref = jnp.broadcast_to(v[3:4], (SUBLANES, w))
        elif s == 2:
            ref = jnp.where(row < 4, jnp.broadcast_to(v[1:2], (SUBLANES, w)),
                            jnp.broadcast_to(v[5:6], (SUBLANES, w)))
        else:
            ref = jnp.where((row & 1) == 1, pltpu.roll(v, 1, 0), v)
        parts.append(ref)
    return jnp.concatenate(parts, axis=0)


_NT = (((1,), (1,)), ((), ()))
_TN = (((0,), (0,)), ((), ()))


def _hgrn_unit(q_ref, f_ref, i_ref, g_ref, lb_ref, gain_ref, o_ref, st_scr, *, hb, tb):
    w = hb * HEAD
    lb = lb_ref[...]
    gain = gain_ref[...]
    ri = lax.broadcasted_iota(jnp.int32, (CHUNK, CHUNK), 0)
    ci = lax.broadcasted_iota(jnp.int32, (CHUNK, CHUNK), 1)
    roww = lax.broadcasted_iota(jnp.int32, (CHUNK, w), 0)
    levels = (32, 16, 8, 4, 2, 1)
    keeps = [(ri == ci).astype(F32)] + [
        None if 2 * s == CHUNK else ((ri & -(2 * s)) == (ci & -(2 * s))).astype(F32) for s in levels]
    second = [(roww & s) != 0 for s in levels]

    heads = [slice(h * HEAD, (h + 1) * HEAD) for h in range(hb)]

    prep = []
    for c in range(tb // CHUNK):
        rows = slice(c * CHUNK, (c + 1) * CHUNK)
        q, k, lf, fg = _hgrn_gates(q_ref[rows, :], f_ref[rows, :], lb)
        cum = _cumsum_rows(lf)
        last = jnp.broadcast_to(cum[CHUNK - 1:CHUNK], (CHUNK, w))
        q_in = (q * jnp.exp2(cum)).astype(BF16)
        k_out = (k * jnp.exp2(last - cum)).astype(BF16)
        qs, ks = [q.astype(BF16)], [k.astype(BF16)]
        for s, m in zip(levels, second):
            if s == 1:
                qs.append(jnp.where(m, q * fg, 0.0).astype(BF16))
                ks.append(jnp.where(m, 0.0, k).astype(BF16))
                continue
            ref = _level_reference(cum, s)
            e = jnp.exp2(jnp.where(m, cum - ref, ref - cum))
            qk = jnp.where(m, q, k) * e
            qs.append(jnp.where(m, qk, 0.0).astype(BF16))
            ks.append(jnp.where(m, 0.0, qk).astype(BF16))
        prep.append((q_in, k_out, i_ref[rows, :].astype(BF16), qs, ks, last))

    scores = []
    for _, _, _, qs, ks, _ in prep:
        per_head = []
        for sl in heads:
            a = None
            for keep, qt, kt in zip(keeps, qs, ks):
                p = lax.dot_general(qt[:, sl], kt[:, sl], _NT, preferred_element_type=F32)
                if keep is not None:
                    p = p * keep
                a = p if a is None else a + p
            per_head.append(a.astype(BF16))
        scores.append(per_head)

    state = [st_scr[h] for h in range(hb)]
    for c, (q_in, k_out, vb, _, _, last) in enumerate(prep):
        rows = slice(c * CHUNK, (c + 1) * CHUNK)
        outs = []
        for h, sl in enumerate(heads):
            o = lax.dot_general(q_in[:, sl], state[h].astype(BF16), _NT, preferred_element_type=F32)
            o = o + jnp.dot(scores[c][h], vb[:, sl], preferred_element_type=F32)
            state[h] = state[h] * jnp.exp2(last[0:1, sl]) + lax.dot_general(
                vb[:, sl], k_out[:, sl], _TN, preferred_element_type=F32)
            outs.append(o * lax.rsqrt(jnp.mean(o * o, axis=-1, keepdims=True) + EPS))
        graw = g_ref[rows, :]
        y = jnp.concatenate(outs, axis=1) * gain * (graw * _sigmoid(graw))
        o_ref[rows, :] = y.astype(o_ref.dtype)
    for h in range(hb):
        st_scr[h] = state[h]


def _hgrn_kernel(q_ref, f_ref, i_ref, g_ref, lb_ref, gain_ref, s0_ref, o_ref, st_scr, *, hb, tb):
    @pl.when(pl.program_id(2) == 0)
    def _():
        st_scr[...] = s0_ref[...]

    _hgrn_unit(q_ref, f_ref, i_ref, g_ref, lb_ref, gain_ref, o_ref, st_scr, hb=hb, tb=tb)


def _hgrn(proj, lb, gain, s0, *, nb, hw, hb_want=4, tb_want=256):
    m = proj.shape[0]
    t = m // nb
    nh = hw // HEAD
    hb = _tile(nh, hb_want)
    w = hb * HEAD
    tb = _tile(t, tb_want)
    nt = t // tb
    ng = nh // hb

    def col(group):
        return lambda b, hg, tt: (b * nt + tt, group * ng + hg)

    return pl.pallas_call(
        functools.partial(_hgrn_kernel, hb=hb, tb=tb),
        grid=(nb, ng, nt),
        in_specs=[
            pl.BlockSpec((tb, w), col(0)),
            pl.BlockSpec((tb, w), col(1)),
            pl.BlockSpec((tb, w), col(2)),
            pl.BlockSpec((tb, w), col(3)),
            pl.BlockSpec((1, w), lambda b, hg, tt: (0, hg)),
            pl.BlockSpec((1, w), lambda b, hg, tt: (0, hg)),
            pl.BlockSpec((hb, HEAD, HEAD), lambda b, hg, tt: (hg, 0, 0)),
        ],
        out_specs=pl.BlockSpec((tb, w), lambda b, hg, tt: (b * nt + tt, hg)),
        out_shape=jax.ShapeDtypeStruct((m, hw), BF16),
        scratch_shapes=[pltpu.VMEM((hb, HEAD, HEAD), F32)],
        compiler_params=_params("parallel", "parallel", "arbitrary"),
        name="hgrn",
    )(proj, proj, proj, proj, lb, gain, s0)


def _hgrn_meta_kernel(f_ref, i_ref, lb_ref, s_ref, *, nh):
    f_raw = f_ref[...]
    _, k, lf, _ = _hgrn_gates(f_raw, f_raw, lb_ref[...])
    cum = _cumsum_rows(lf)
    r, w = cum.shape
    last = jnp.broadcast_to(cum[r - 1:r], (r, w))
    k_out = (k * jnp.exp2(last - cum)).astype(BF16)
    vb = i_ref[...].astype(BF16)
    for h in range(nh):
        sl = slice(h * HEAD, (h + 1) * HEAD)
        s_ref[h] = lax.dot_general(vb[:, sl], k_out[:, sl], _TN, preferred_element_type=F32)


def _hgrn_meta(proj_meta, lb, *, hw):
    nmeta = proj_meta.shape[0]
    nh = hw // HEAD
    return pl.pallas_call(
        functools.partial(_hgrn_meta_kernel, nh=nh),
        grid=(1,),
        in_specs=[
            pl.BlockSpec((nmeta, hw), lambda i: (0, 1)),
            pl.BlockSpec((nmeta, hw), lambda i: (0, 2)),
            pl.BlockSpec((1, hw), lambda i: (0, 0)),
        ],
        out_specs=pl.BlockSpec((nh, HEAD, HEAD), lambda i: (0, 0, 0)),
        out_shape=jax.ShapeDtypeStruct((nh, HEAD, HEAD), F32),
        compiler_params=_params("arbitrary"),
        name="hgrn_meta",
    )(proj_meta, proj_meta, lb)


def _lru_unit(x_ref, g_ref, cw_ref, cb_ref, wa_ref, ba_ref, wx_ref, bx_ref, lam_ref, gain_ref, y_ref,
              xbuf, a_scr, b_scr, hcar, *, tb, nblk, blk):
    c = x_ref.shape[1]
    nlam = -lam_ref[...]
    sp = jnp.maximum(nlam, 0.0) + jnp.log(1.0 + jnp.exp(-jnp.abs(nlam)))
    rate = (-RG_C * LOG2E) * sp
    for hh in range(nblk):
        sl = slice(hh * blk, (hh + 1) * blk)
        x = x_ref[:, sl]
        xbuf[SUBLANES:SUBLANES + tb, sl] = x
        xc = (cb_ref[:, sl] + cw_ref[3:4, sl] * x
              + cw_ref[2:3, sl] * xbuf[SUBLANES - 1:SUBLANES - 1 + tb, sl]
              + cw_ref[1:2, sl] * xbuf[SUBLANES - 2:SUBLANES - 2 + tb, sl]
              + cw_ref[0:1, sl] * xbuf[SUBLANES - 3:SUBLANES - 3 + tb, sl])
        xbuf[0:SUBLANES, sl] = x[tb - SUBLANES:tb, :]
        xcb = xc.astype(BF16)
        r = _sigmoid(jnp.dot(xcb, wa_ref[hh], preferred_element_type=F32) + ba_ref[:, sl])
        ig = _sigmoid(jnp.dot(xcb, wx_ref[hh], preferred_element_type=F32) + bx_ref[:, sl])
        a = jnp.exp2(r * rate[:, sl])
        a_scr[:, sl] = a
        om = 1.0 - a * a
        root = jnp.where(om > 0.0, om * lax.rsqrt(om), 0.0)
        b_scr[:, sl] = root * (ig * xc)

    row = lax.broadcasted_iota(jnp.int32, (SUBLANES, c), 0)

    def group(gi, hprev):
        r0 = pl.multiple_of(gi * SUBLANES, SUBLANES)
        av = a_scr[pl.ds(r0, SUBLANES), :]
        bv = b_scr[pl.ds(r0, SUBLANES), :]
        for dd in (1, 2, 4):
            keep = row >= dd
            bv = jnp.where(keep, av * pltpu.roll(bv, dd, 0) + bv, bv)
            av = jnp.where(keep, av * pltpu.roll(av, dd, 0), av)
        hv = av * hprev + bv
        b_scr[pl.ds(r0, SUBLANES), :] = hv
        return jnp.broadcast_to(hv[SUBLANES - 1:SUBLANES], (SUBLANES, c))

    hlast = lax.fori_loop(0, tb // SUBLANES, group, hcar[...])
    hcar[...] = hlast

    blocks = [slice(hh * blk, (hh + 1) * blk) for hh in range(nblk)]
    ssq = sum(jnp.sum(b_scr[:, sl] * b_scr[:, sl], axis=-1, keepdims=True) for sl in blocks)
    rstd = lax.rsqrt(ssq * (1.0 / c) + EPS)
    for sl in blocks:
        gr = g_ref[:, sl]
        half = 0.5 * gr
        gelu = half + half * jnp.tanh(gr * (GELU_K + (GELU_K * 0.044715) * (gr * gr)))
        y_ref[:, sl] = (b_scr[:, sl] * rstd * gain_ref[:, sl] * gelu).astype(y_ref.dtype)
    return hlast


def _lru_kernel(*refs, tb, nblk, blk, nblks):
    (x_ref, g_ref, cw_ref, cb_ref, wa_ref, ba_ref, wx_ref, bx_ref, lam_ref, gain_ref, h0_ref, xt0_ref) = refs[:12]
    ns = len(nblks)
    side_in = refs[12:12 + ns]
    y_ref, hl_ref = refs[12 + ns:14 + ns]
    side_out = refs[14 + ns:14 + 2 * ns]
    xbuf, a_scr, b_scr, hcar = refs[-4:]
    t = pl.program_id(1)

    @pl.when(t == 0)
    def _():
        hcar[...] = h0_ref[...]
        xbuf[0:SUBLANES, :] = xt0_ref[...]

    _convert_side(pl.program_id(0) * pl.num_programs(1) + t, side_in, side_out, nblks)
    hlast = _lru_unit(x_ref, g_ref, cw_ref, cb_ref, wa_ref, ba_ref, wx_ref, bx_ref, lam_ref, gain_ref, y_ref,
                      xbuf, a_scr, b_scr, hcar, tb=tb, nblk=nblk, blk=blk)

    @pl.when(t == pl.num_programs(1) - 1)
    def _():
        hl_ref[0] = hlast


def _lru(proj, cw, cb, wa, ba, wx, bx, lam, gain, h0, xt0, side=(), *, nb, lw, xcol, tb_want=256):
    m = proj.shape[0]
    t = m // nb
    tb = _tile(t, tb_want)
    nt = t // tb
    nblk, blk, _ = wa.shape
    vec = pl.BlockSpec((1, lw), lambda b, tt: (0, 0))
    side_specs, nblks = _side_specs(side, nb * nt, lambda b, tt: b * nt + tt)
    res = pl.pallas_call(
        functools.partial(_lru_kernel, tb=tb, nblk=nblk, blk=blk, nblks=tuple(nblks)),
        grid=(nb, nt),
        in_specs=[
            pl.BlockSpec((tb, lw), lambda b, tt: (b * nt + tt, xcol)),
            pl.BlockSpec((tb, lw), lambda b, tt: (b * nt + tt, xcol + 1)),
            pl.BlockSpec((cw.shape[0], lw), lambda b, tt: (0, 0)),
            vec,
            pl.BlockSpec((nblk, blk, blk), lambda b, tt: (0, 0, 0)),
            vec,
            pl.BlockSpec((nblk, blk, blk), lambda b, tt: (0, 0, 0)),
            vec, vec, vec,
            pl.BlockSpec((SUBLANES, lw), lambda b, tt: (0, 0)),
            pl.BlockSpec((SUBLANES, lw), lambda b, tt: (0, 0)),
        ] + side_specs,
        out_specs=[
            pl.BlockSpec((tb, lw), lambda b, tt: (b * nt + tt, 0)),
            pl.BlockSpec((1, SUBLANES, lw), lambda b, tt: (b, 0, 0)),
        ] + side_specs,
        out_shape=[
            jax.ShapeDtypeStruct((m, lw), BF16),
            jax.ShapeDtypeStruct((nb, SUBLANES, lw), F32),
        ] + _side_shapes(side),
        scratch_shapes=[
            pltpu.VMEM((SUBLANES + tb, lw), F32),
            pltpu.VMEM((tb, lw), F32),
            pltpu.VMEM((tb, lw), F32),
            pltpu.VMEM((SUBLANES, lw), F32),
        ],
        compiler_params=_params("arbitrary", "arbitrary"),
        name="lru",
    )(proj, proj, cw, cb, wa, ba, wx, bx, lam, gain, h0, xt0, *side)
    return res[0], res[1], tuple(res[2:])


def kernel(x, meta_tokens, ffn1_pre_norm, ffn1_w_gate, ffn1_w_up, ffn1_w_down, ffn1_post_norm, mix_pre_norm, w_in, hgrn_lb_logits, hgrn_out_norm, lru_conv_w, lru_conv_b, lru_w_a, lru_b_a, lru_w_x, lru_b_x, lru_lambda, lru_out_norm, w_out, mix_post_norm, ffn2_pre_norm, ffn2_w_gate, ffn2_w_up, ffn2_w_down, ffn2_post_norm):
    nb, seq, d = x.shape
    hw = hgrn_out_norm.shape[1]
    lw = lru_out_norm.shape[1]
    assert ffn1_pre_norm.shape[0] == 1, "one layer"
    assert hw == lw and w_in.shape[2] == 4 * hw + 2 * lw and hw % HEAD == 0
    assert seq % CHUNK == 0 and meta_tokens.shape[0] % SUBLANES == 0

    def vec(p):
        return p[0].reshape(1, -1).astype(F32)

    lb = jnp.cumsum(jax.nn.softmax(hgrn_lb_logits.astype(F32), axis=0), axis=0)[0].reshape(1, hw)
    lru_p = (lru_conv_w[0].astype(F32), vec(lru_conv_b), lru_w_a[0].astype(BF16), vec(lru_b_a),
             lru_w_x[0].astype(BF16), vec(lru_b_x), vec(lru_lambda), vec(lru_out_norm))
    xcol = (4 * hw) // lw

    hm, ffn1_w = _ffn_convert(meta_tokens.astype(F32), vec(ffn1_pre_norm), ffn1_w_gate[0], ffn1_w_up[0],
                              ffn1_w_down[0], vec(ffn1_post_norm))
    pm, w_in_b, _ = _inproj(hm, vec(mix_pre_norm), w_in[0], bn_want=512)
    s_meta = _hgrn_meta(pm, lb, hw=hw)
    zeros8 = jnp.zeros((SUBLANES, lw), F32)
    _, h_meta, _ = _lru(pm, *lru_p, zeros8, zeros8, nb=1, lw=lw, xcol=xcol)
    xt_meta = pm[pm.shape[0] - SUBLANES:, 4 * hw:4 * hw + lw]

    h = x.reshape(nb * seq, d).astype(F32)
    h = _ffn(h, vec(ffn1_pre_norm), *ffn1_w, vec(ffn1_post_norm))
    proj, _, ffn2_w = _inproj(h, vec(mix_pre_norm), w_in_b, (ffn2_w_gate[0], ffn2_w_up[0], ffn2_w_down[0]))
    y_h = _hgrn(proj, lb, vec(hgrn_out_norm), s_meta, nb=nb, hw=hw)
    y_l, _, (w_out_b,) = _lru(proj, *lru_p, h_meta[0], xt_meta, (w_out[0],), nb=nb, lw=lw, xcol=xcol)
    h = _outproj(y_h, y_l, w_out_b, h, vec(mix_post_norm))
    h = _ffn(h, vec(ffn2_pre_norm), *ffn2_w, vec(ffn2_post_norm))
    return h.reshape(nb, seq, d).astype(x.dtype)
```
